```python
import math
import jax, jax.numpy as jnp
from jax import lax
import numpy as np

D_MODEL = 2048
BATCH = 4
SEQ = 2048
DEPTH = 4
DEC_BATCH = 8
DEC_SEQ = 4
PAST_LEN = 16384
PAGE_SIZE = 128

N_GROUPS = 4
W_GROUP = D_MODEL // N_GROUPS
N_HEADS_A = 4
HEAD_DIM_A = W_GROUP // (2 * N_HEADS_A)
V_DIM_A = 2 * HEAD_DIM_A
ROPE_THETA = 10000.0
Q_BLOCK = 128
N_HEADS_B = 4
BLK_B = W_GROUP // N_HEADS_B
CONV_B = 4
LRU_C = 8.0
CONV_C = 31
CONV_D = 3
N_IN_SLOTS = 10
IN_WIDTH = N_IN_SLOTS * W_GROUP
D_FF = 5632
NORM_EPS = 1e-6

kernel_name = "hybrid_diffattn_rglru_conformer_shortconv_step"


def _rms(x, w):
    xf = x.astype(jnp.float32)
    y = xf * lax.rsqrt(jnp.mean(xf * xf, axis=-1, keepdims=True) + NORM_EPS)
    return (y * w.astype(jnp.float32)).astype(x.dtype)


def _layernorm(x, w, b):
    xf = x.astype(jnp.float32)
    mu = jnp.mean(xf, axis=-1, keepdims=True)
    xc = xf - mu
    var = jnp.mean(xc * xc, axis=-1, keepdims=True)
    y = xc * lax.rsqrt(var + NORM_EPS) * w.astype(jnp.float32) + b.astype(jnp.float32)
    return y.astype(x.dtype)


def _swiglu(x, w_gu, w_down):
    g, u = jnp.split(x @ w_gu, 2, axis=-1)
    return (jax.nn.silu(g) * u) @ w_down


def _rope(x, pos):
    half = HEAD_DIM_A // 2
    inv = ROPE_THETA ** (-jnp.arange(half, dtype=jnp.float32) / half)
    ang = pos.astype(jnp.float32)[:, None] * inv[None, :]
    cos = jnp.cos(ang)[:, None, None, :]
    sin = jnp.sin(ang)[:, None, None, :]
    xf = x.astype(jnp.float32)
    x1, x2 = xf[..., :half], xf[..., half:]
    out = jnp.concatenate([x1 * cos - x2 * sin, x2 * cos + x1 * sin], axis=-1)
    return out.astype(x.dtype)


def _diff_attn(q, k, v, q_pos, k_pos, lam):
    s = jnp.einsum('bqhcd,bkhcd->bhcqk', q.astype(jnp.float32), k.astype(jnp.float32))
    s = s * (HEAD_DIM_A ** -0.5)
    mask = k_pos[None, :] <= q_pos[:, None]
    s = jnp.where(mask, s, -jnp.inf)
    p = jax.nn.softmax(s, axis=-1)
    w = p[:, :, 0] - lam * p[:, :, 1]
    return jnp.einsum('bhqk,bkhe->bqhe', w, v.astype(jnp.float32))


def _attn_prompt(q, k, v, lam):
    B, T = q.shape[:2]
    nb = T // Q_BLOCK
    qb = q.reshape(B, nb, Q_BLOCK, N_HEADS_A, 2, HEAD_DIM_A).transpose(1, 0, 2, 3, 4, 5)
    k_pos = jnp.arange(T)

    def blk(args):
        qi, i = args
        q_pos = i * Q_BLOCK + jnp.arange(Q_BLOCK)
        return _diff_attn(qi, k, v, q_pos, k_pos, lam)

    o = lax.map(blk, (qb, jnp.arange(nb)))
    return o.transpose(1, 0, 2, 3, 4).reshape(B, T, N_HEADS_A, V_DIM_A)


def _causal_dwconv(u, prev, w, b):
    z = jnp.concatenate([prev.astype(u.dtype), u], axis=1)
    y = lax.conv_general_dilated(z, w[:, None, :].astype(u.dtype), (1,), 'VALID',
                                 dimension_numbers=('NWC', 'WIO', 'NWC'),
                                 feature_group_count=u.shape[-1])
    if b is not None:
        y = y + b.astype(u.dtype)
    return y, z[:, -(w.shape[0] - 1):]


def _rglru(x, h0, gate_w, gate_b, a_param):
    B, T, W = x.shape
    xf = x.astype(jnp.float32)
    xb = xf.reshape(B, T, N_HEADS_B, BLK_B)
    g = jnp.einsum('btnk,gnkj->gbtnj', xb, gate_w.astype(jnp.float32)).reshape(2, B, T, W)
    g = g + gate_b.astype(jnp.float32)[:, None, None, :]
    r = jax.nn.sigmoid(g[0])
    i = jax.nn.sigmoid(g[1])
    log_a = -LRU_C * r * jax.nn.softplus(-a_param.astype(jnp.float32))
    a = jnp.exp(log_a)
    u = jnp.sqrt(-jnp.expm1(2.0 * log_a)) * (i * xf)

    def step(h, inp):
        a_t, u_t = inp
        h = a_t * h + u_t
        return h, h

    h_T, hs = lax.scan(step, h0.astype(jnp.float32), (a.transpose(1, 0, 2), u.transpose(1, 0, 2)))
    return hs.transpose(1, 0, 2).astype(x.dtype), h_T.astype(x.dtype)


def _layer(x, pos, attend, st, lam_init, P):
    nw = P['norm_w']
    B, T = x.shape[:2]
    x = x + 0.5 * _rms(_swiglu(_rms(x, nw[0]), P['ffn1_w_gu'], P['ffn1_w_down']), nw[1])
    hn = _rms(x, nw[2])
    z = hn @ P['w_in']
    q, k, v, xb, gb, ca, cb, sb, sc, sx = jnp.split(z, N_IN_SLOTS, axis=-1)
    q = _rope(q.reshape(B, T, N_HEADS_A, 2, HEAD_DIM_A), pos)
    k = _rope(k.reshape(B, T, N_HEADS_A, 2, HEAD_DIM_A), pos)
    v = v.reshape(B, T, N_HEADS_A, V_DIM_A)
    lp = P['attn_lambda'].astype(jnp.float32)
    lam = jnp.exp(jnp.sum(lp[0] * lp[1])) - jnp.exp(jnp.sum(lp[2] * lp[3])) + lam_init
    oa = attend(q, k, v, lam)
    oa = _rms(oa, P['attn_subln_w']) * (1.0 - lam_init)
    ya = oa.reshape(B, T, W_GROUP).astype(x.dtype)
    xc, lru_buf = _causal_dwconv(xb, st[1], P['lru_conv_w'], P['lru_conv_b'])
    yr, h_T = _rglru(xc, st[0], P['lru_gate_w'], P['lru_gate_b'], P['lru_a_param'])
    yb = jax.nn.gelu(gb) * yr
    uc, cm_buf = _causal_dwconv(ca * jax.nn.sigmoid(cb), st[2], P['cm_conv_w'], P['cm_conv_b'])
    yc = jax.nn.silu(_layernorm(uc, P['cm_ln_w'], P['cm_ln_b']))
    sconv, sc_buf = _causal_dwconv(sc * sx, st[3], P['sc_conv_w'], None)
    yd = sb * sconv
    y = jnp.concatenate([ya, yb, yc, yd], axis=-1) @ P['w_out']
    x = x + _rms(y, nw[3])
    x = x + 0.5 * _rms(_swiglu(_rms(x, nw[4]), P['ffn2_w_gu'], P['ffn2_w_down']), nw[5])
    k_rows = k.reshape(B, T, N_HEADS_A, 2 * HEAD_DIM_A)
    return x, (k_rows, v, h_T, lru_buf, cm_buf, sc_buf)


def setup_inputs(seed: int = 0) -> dict:
    key = jax.random.key(seed)
    ks = iter(jax.random.split(key, 40))

    def nrm(shape, scale):
        return jax.random.normal(next(ks), shape, jnp.float32) * scale

    n_pages = PAST_LEN // PAGE_SIZE
    n_used = DEC_BATCH * n_pages
    n_pool = (5 * n_used) // 4
    W = W_GROUP
    kv_shape = (DEPTH, n_pool, PAGE_SIZE, N_HEADS_A, 2 * HEAD_DIM_A)
    x_prompt = nrm((BATCH, SEQ, D_MODEL), 1.0)
    x_sample = nrm((DEC_BATCH, DEC_SEQ, D_MODEL), 1.0)
    cache_k = nrm(kv_shape, 1.0)
    cache_v = nrm(kv_shape, 1.0)
    state_lru_h = nrm((DEPTH, DEC_BATCH, W), 0.5)
    state_lru_conv = nrm((DEPTH, DEC_BATCH, CONV_B - 1, W), 1.0)
    state_cm_conv = nrm((DEPTH, DEC_BATCH, CONV_C - 1, W), 1.0)
    state_sc_conv = nrm((DEPTH, DEC_BATCH, CONV_D - 1, W), 1.0)
    page_table = jax.random.permutation(next(ks), n_pool)[:n_used].reshape(DEC_BATCH, n_pages).astype(jnp.int32)
    norm_w = 1.0 + nrm((DEPTH, 6, D_MODEL), 0.02)
    ffn1_w_gu = nrm((DEPTH, D_MODEL, 2 * D_FF), D_MODEL ** -0.5)
    ffn1_w_down = nrm((DEPTH, D_FF, D_MODEL), D_FF ** -0.5)
    ffn2_w_gu = nrm((DEPTH, D_MODEL, 2 * D_FF), D_MODEL ** -0.5)
    ffn2_w_down = nrm((DEPTH, D_FF, D_MODEL), D_FF ** -0.5)
    w_in = nrm((DEPTH, D_MODEL, IN_WIDTH), D_MODEL ** -0.5)
    w_out = nrm((DEPTH, N_GROUPS * W_GROUP, D_MODEL), (N_GROUPS * W_GROUP) ** -0.5)
    attn_lambda = nrm((DEPTH, 4, HEAD_DIM_A), 0.1)
    attn_subln_w = 1.0 + nrm((DEPTH, V_DIM_A), 0.02)
    lru_conv_w = nrm((DEPTH, CONV_B, W), CONV_B ** -0.5)
    lru_conv_b = nrm((DEPTH, W), 0.02)
    lru_gate_w = nrm((DEPTH, 2, N_HEADS_B, BLK_B, BLK_B), BLK_B ** -0.5)
    lru_gate_b = nrm((DEPTH, 2, W), 0.02)
    a8 = jax.random.uniform(next(ks), (DEPTH, W), jnp.float32, 0.9, 0.999)
    a = a8 ** (1.0 / LRU_C)
    lru_a_param = jnp.log(a) - jnp.log1p(-a)
    cm_conv_w = nrm((DEPTH, CONV_C, W), CONV_C ** -0.5)
    cm_conv_b = nrm((DEPTH, W), 0.02)
    cm_ln_w = 1.0 + nrm((DEPTH, W), 0.02)
    cm_ln_b = nrm((DEPTH, W), 0.02)
    sc_conv_w = nrm((DEPTH, CONV_D, W), CONV_D ** -0.5)
    return {"x_prompt": x_prompt, "x_sample": x_sample, "cache_k": cache_k, "cache_v": cache_v,
            "state_lru_h": state_lru_h, "state_lru_conv": state_lru_conv,
            "state_cm_conv": state_cm_conv, "state_sc_conv": state_sc_conv, "page_table": page_table,
            "norm_w": norm_w, "ffn1_w_gu": ffn1_w_gu, "ffn1_w_down": ffn1_w_down,
            "ffn2_w_gu": ffn2_w_gu, "ffn2_w_down": ffn2_w_down, "w_in": w_in, "w_out": w_out,
            "attn_lambda": attn_lambda, "attn_subln_w": attn_subln_w,
            "lru_conv_w": lru_conv_w, "lru_conv_b": lru_conv_b, "lru_gate_w": lru_gate_w,
            "lru_gate_b": lru_gate_b, "lru_a_param": lru_a_param,
            "cm_conv_w": cm_conv_w, "cm_conv_b": cm_conv_b, "cm_ln_w": cm_ln_w, "cm_ln_b": cm_ln_b,
            "sc_conv_w": sc_conv_w}


def reference(x_prompt, x_sample, cache_k, cache_v, state_lru_h, state_lru_conv, state_cm_conv,
              state_sc_conv, page_table, norm_w, ffn1_w_gu, ffn1_w_down, ffn2_w_gu, ffn2_w_down,
              w_in, w_out, attn_lambda, attn_subln_w, lru_conv_w, lru_conv_b, lru_gate_w, lru_gate_b,
              lru_a_param, cm_conv_w, cm_conv_b, cm_ln_w, cm_ln_b, sc_conv_w):
    B, T = x_prompt.shape[:2]
    DB, TS = x_sample.shape[:2]
    n_pages = page_table.shape[1]
    past = n_pages * cache_k.shape[2]
    pos_p = jnp.arange(T)
    pos_s = past + jnp.arange(TS)
    k_pos_s = jnp.arange(past + TS)
    dt = x_prompt.dtype
    xp, xs = x_prompt, x_sample
    outs_p, outs_s = [], []
    for l in range(DEPTH):
        P = {'norm_w': norm_w[l], 'ffn1_w_gu': ffn1_w_gu[l], 'ffn1_w_down': ffn1_w_down[l],
             'ffn2_w_gu': ffn2_w_gu[l], 'ffn2_w_down': ffn2_w_down[l], 'w_in': w_in[l], 'w_out': w_out[l],
             'attn_lambda': attn_lambda[l], 'attn_subln_w': attn_subln_w[l],
             'lru_conv_w': lru_conv_w[l], 'lru_conv_b': lru_conv_b[l], 'lru_gate_w': lru_gate_w[l],
             'lru_gate_b': lru_gate_b[l], 'lru_a_param': lru_a_param[l],
             'cm_conv_w': cm_conv_w[l], 'cm_conv_b': cm_conv_b[l], 'cm_ln_w': cm_ln_w[l],
             'cm_ln_b': cm_ln_b[l], 'sc_conv_w': sc_conv_w[l]}
        lam_init = 0.8 - 0.6 * math.exp(-0.3 * l)
        st_p = (jnp.zeros((B, W_GROUP), dt), jnp.zeros((B, CONV_B - 1, W_GROUP), dt),
                jnp.zeros((B, CONV_C - 1, W_GROUP), dt), jnp.zeros((B, CONV_D - 1, W_GROUP), dt))
        xp, sp = _layer(xp, pos_p, _attn_prompt, st_p, lam_init, P)
        outs_p.append(sp)
        k_past = cache_k[l][page_table].reshape(DB, past, N_HEADS_A, 2, HEAD_DIM_A)
        v_past = cache_v[l][page_table].reshape(DB, past, N_HEADS_A, V_DIM_A)

        def attend_s(q, k, v, lam, k_past=k_past, v_past=v_past):
            k_all = jnp.concatenate([k_past.astype(k.dtype), k], axis=1)
            v_all = jnp.concatenate([v_past.astype(v.dtype), v], axis=1)
            return _diff_attn(q, k_all, v_all, pos_s, k_pos_s, lam)

        st_s = (state_lru_h[l], state_lru_conv[l], state_cm_conv[l], state_sc_conv[l])
        xs, ss = _layer(xs, pos_s, attend_s, st_s, lam_init, P)
        outs_s.append(ss)
    k_p = jnp.stack([o[0] for o in outs_p])
    v_p = jnp.stack([o[1] for o in outs_p])
    h_p = jnp.stack([o[2] for o in outs_p])
    lc_p = jnp.stack([o[3] for o in outs_p])
    cc_p = jnp.stack([o[4] for o in outs_p])
    sc_p = jnp.stack([o[5] for o in outs_p])
    k_s = jnp.stack([o[0] for o in outs_s])
    v_s = jnp.stack([o[1] for o in outs_s])
    h_s = jnp.stack([o[2] for o in outs_s])
    lc_s = jnp.stack([o[3] for o in outs_s])
    cc_s = jnp.stack([o[4] for o in outs_s])
    sc_s = jnp.stack([o[5] for o in outs_s])
    return (xp, xs, k_p, v_p, h_p, lc_p, cc_p, sc_p, k_s, v_s, h_s, lc_s, cc_s, sc_s)
```

```python
import functools
import math

import jax
import jax.numpy as jnp
from jax import lax
from jax.experimental import pallas as pl
from jax.experimental.pallas import tpu as pltpu

F32 = jnp.float32
BF16 = jnp.bfloat16

NORM_EPS = 1e-6
ROPE_THETA = 10000.0
LRU_C = 8.0
N_GROUPS = 4
N_IN_SLOTS = 10
VMEM_LIMIT_BYTES = 56 * 1024 * 1024
LANES = 128
SUBLANES = 8


def _cparams(*sem):
    return pltpu.CompilerParams(dimension_semantics=sem, vmem_limit_bytes=VMEM_LIMIT_BYTES)


def _pick_tile(n, target, quantum):
    if n <= target:
        return n
    t = (target // quantum) * quantum
    while t >= quantum:
        if n % t == 0:
            return t
        t -= quantum
    return n


def _dot(a, b):
    return jnp.dot(a, b, preferred_element_type=F32)


def _dot_nt(a, b):
    return lax.dot_general(a, b, (((1,), (1,)), ((), ())), preferred_element_type=F32)


def _rms_rows(x, w_row):
    ms = jnp.mean(x * x, axis=-1, keepdims=True)
    return x * lax.rsqrt(ms + NORM_EPS) * w_row


def _norm_matmul_kernel(x_ref, nw_ref, *refs, norm_row, swiglu):
    if swiglu:
        wg_ref, wu_ref, out_ref, xn_ref = refs
    else:
        w_ref, out_ref, xn_ref = refs

    @pl.when(pl.program_id(1) == 0)
    def _():
        xn_ref[...] = _rms_rows(x_ref[...], nw_ref[norm_row:norm_row + 1, :]).astype(BF16)

    xn = xn_ref[...]
    if swiglu:
        g = _dot(xn, wg_ref[...])
        u = _dot(xn, wu_ref[...])
        out_ref[...] = (g * jax.nn.sigmoid(g) * u).astype(out_ref.dtype)
    else:
        out_ref[...] = _dot(xn, w_ref[...]).astype(out_ref.dtype)


def _norm_matmul(x, norm_w, w, layer, norm_row, *, swiglu, out_dtype):
    m, d = x.shape
    n = w.shape[2]
    n_out = n // 2 if swiglu else n
    tm = _pick_tile(m, 1024, 16)
    tn = _pick_tile(n_out, 512, LANES)
    n_blocks = n_out // tn
    in_specs = [
        pl.BlockSpec((tm, d), lambda i, j: (i, 0)),
        pl.BlockSpec((None, norm_w.shape[1], d), lambda i, j: (layer, 0, 0)),
        pl.BlockSpec((None, d, tn), lambda i, j: (layer, 0, j)),
    ]
    args = [x, norm_w, w]
    if swiglu:
        in_specs.append(pl.BlockSpec((None, d, tn), lambda i, j: (layer, 0, j + n_blocks)))
        args.append(w)
    return pl.pallas_call(
        functools.partial(_norm_matmul_kernel, norm_row=norm_row, swiglu=swiglu),
        grid=(m // tm, n_blocks),
        in_specs=in_specs,
        out_specs=pl.BlockSpec((tm, tn), lambda i, j: (i, j)),
        out_shape=jax.ShapeDtypeStruct((m, n_out), out_dtype),
        scratch_shapes=[pltpu.VMEM((tm, d), BF16)],
        compiler_params=_cparams("parallel", "arbitrary"),
        name="norm_matmul_swiglu" if swiglu else "norm_matmul",
    )(*args)


def _matmul_norm_res_kernel(*refs, n_parts, part_rows, norm_row, scale):
    h_refs = refs[:n_parts]
    w_ref, x_ref, nw_ref, out_ref = refs[n_parts:]
    y = None
    row = 0
    for h_ref, rows in zip(h_refs, part_rows):
        part = _dot(h_ref[...], w_ref[row:row + rows, :])
        y = part if y is None else y + part
        row += rows
    yn = _rms_rows(y, nw_ref[norm_row:norm_row + 1, :])
    out_ref[...] = x_ref[...] + scale * yn


def _matmul_norm_res(h_parts, w, x, norm_w, layer, norm_row, scale):
    m, d = x.shape
    k = w.shape[1]
    part_rows = tuple(h.shape[1] for h in h_parts)
    assert sum(part_rows) == k
    tm = _pick_tile(m, 256 if k > 4096 else 512, 16)
    in_specs = [pl.BlockSpec((tm, r), lambda i: (i, 0)) for r in part_rows]
    in_specs += [
        pl.BlockSpec((None, k, d), lambda i: (layer, 0, 0), pipeline_mode=pl.Buffered(1)),
        pl.BlockSpec((tm, d), lambda i: (i, 0)),
        pl.BlockSpec((None, norm_w.shape[1], d), lambda i: (layer, 0, 0)),
    ]
    return pl.pallas_call(
        functools.partial(_matmul_norm_res_kernel, n_parts=len(h_parts), part_rows=part_rows,
                          norm_row=norm_row, scale=scale),
        grid=(m // tm,),
        in_specs=in_specs,
        out_specs=pl.BlockSpec((tm, d), lambda i: (i, 0)),
        out_shape=jax.ShapeDtypeStruct((m, d), F32),
        compiler_params=_cparams("parallel"),
        name="matmul_norm_res",
    )(*h_parts, w, x, norm_w)


def _swap_halves(x, half):
    n = x.shape[-1]
    lane = lax.broadcasted_iota(jnp.int32, x.shape, x.ndim - 1)
    first = (lane & half) == 0
    return jnp.where(first, pltpu.roll(x, n - half, x.ndim - 1), pltpu.roll(x, half, x.ndim - 1))


def _rope_kernel(q_ref, k_ref, v_ref, cos_ref, sin_ref, qo_ref, ko_ref, kb_ref, vb_ref, *, half, q_scale):
    cos = cos_ref[...]
    sin = sin_ref[...]
    q = q_ref[...]
    k = k_ref[...]
    qr = q * cos + _swap_halves(q, half) * sin
    kr = k * cos + _swap_halves(k, half) * sin
    qo_ref[...] = (qr * q_scale).astype(qo_ref.dtype)
    ko_ref[...] = kr
    kb_ref[...] = kr.astype(BF16)
    vb_ref[...] = v_ref[...].astype(BF16)


def _rope(z, cos, sin, *, w, head_dim, q_dtype):
    b, t, _ = z.shape
    tt = _pick_tile(t, 512, 16)
    zspec = lambda slot: pl.BlockSpec((None, tt, w), lambda bi, ti: (bi, ti, slot))
    tspec = pl.BlockSpec((tt, w), lambda bi, ti: (ti, 0))
    ospec = pl.BlockSpec((None, tt, w), lambda bi, ti: (bi, ti, 0))
    return pl.pallas_call(
        functools.partial(_rope_kernel, half=head_dim // 2, q_scale=head_dim ** -0.5),
        grid=(b, t // tt),
        in_specs=[zspec(0), zspec(1), zspec(2), tspec, tspec],
        out_specs=[ospec, ospec, ospec, ospec],
        out_shape=[jax.ShapeDtypeStruct((b, t, w), q_dtype), jax.ShapeDtypeStruct((b, t, w), F32),
                   jax.ShapeDtypeStruct((b, t, w), BF16), jax.ShapeDtypeStruct((b, t, w), BF16)],
        compiler_params=_cparams("parallel", "parallel"),
        name="rope",
    )(z, z, z, cos, sin)


def _rope_tables(pos, w, head_dim):
    half = head_dim // 2
    inv = ROPE_THETA ** (-jnp.arange(half, dtype=F32) / half)
    ang = pos.astype(F32)[:, None] * inv[None, :]
    cos = jnp.cos(ang)
    sin = jnp.sin(ang)
    cos_d = jnp.concatenate([cos, cos], axis=-1)
    sin_d = jnp.concatenate([-sin, sin], axis=-1)
    reps = w // head_dim
    return jnp.tile(cos_d, (1, reps)), jnp.tile(sin_d, (1, reps))


def _lambda_value(lp, lam_init):
    s1 = jnp.sum(lp[0:1, :] * lp[1:2, :], axis=-1, keepdims=True)
    s2 = jnp.sum(lp[2:3, :] * lp[3:4, :], axis=-1, keepdims=True)
    return jnp.exp(s1) - jnp.exp(s2) + lam_init


def _attn_prompt_kernel(q_ref, k_ref, v_ref, lp_ref, sw_ref, o_ref, *, tile, head_dim, lam_init):
    qi = pl.program_id(2)
    q = q_ref[...]
    lane = lax.broadcasted_iota(jnp.int32, q.shape, 1)
    zero = jnp.zeros_like(q)
    q1 = jnp.where(lane < head_dim, q, zero)
    q2 = jnp.where(lane >= head_dim, q, zero)
    vdim = v_ref.shape[-1]

    def update(s, carry, vs):
        m, l, acc = carry
        m_new = jnp.maximum(m, jnp.max(s, axis=-1, keepdims=True))
        alpha = jnp.exp(m - m_new)
        e = jnp.exp(s - m_new)
        l = alpha * l + jnp.sum(e, axis=-1, keepdims=True)
        acc = alpha * acc + _dot(e.astype(BF16), vs)
        return m_new, l, acc

    def step(j, carry, masked):
        c1, c2 = carry
        start = pl.multiple_of(j * tile, tile)
        ks = k_ref[pl.ds(start, tile), :]
        vs = v_ref[pl.ds(start, tile), :]
        s1 = _dot_nt(q1, ks)
        s2 = _dot_nt(q2, ks)
        if masked:
            row = lax.broadcasted_iota(jnp.int32, s1.shape, 0)
            col = lax.broadcasted_iota(jnp.int32, s1.shape, 1)
            keep = col <= row
            s1 = jnp.where(keep, s1, -jnp.inf)
            s2 = jnp.where(keep, s2, -jnp.inf)
        return update(s1, c1, vs), update(s2, c2, vs)

    init = (jnp.full((tile, 1), -jnp.inf, F32), jnp.zeros((tile, 1), F32), jnp.zeros((tile, vdim), F32))
    carry = lax.fori_loop(0, qi, lambda j, c: step(j, c, False), (init, init))
    (_, l1, a1), (_, l2, a2) = step(qi, carry, True)
    lam = _lambda_value(lp_ref[...], lam_init)
    o = a1 / l1 - lam * (a2 / l2)
    o_ref[...] = (_rms_rows(o, sw_ref[...]) * (1.0 - lam_init)).astype(o_ref.dtype)


def _attn_prompt(q, k, v, attn_lambda, subln_w, layer, lam_init, *, n_heads, head_dim):
    b, t, w = q.shape
    e = 2 * head_dim
    tile = _pick_tile(t, 256, LANES)
    return pl.pallas_call(
        functools.partial(_attn_prompt_kernel, tile=tile, head_dim=head_dim, lam_init=lam_init),
        grid=(b, n_heads, t // tile),
        in_specs=[
            pl.BlockSpec((None, tile, e), lambda bi, h, qi: (bi, qi, h)),
            pl.BlockSpec((None, t, e), lambda bi, h, qi: (bi, 0, h)),
            pl.BlockSpec((None, t, e), lambda bi, h, qi: (bi, 0, h)),
            pl.BlockSpec((None, 4, head_dim), lambda bi, h, qi: (layer, 0, 0)),
            pl.BlockSpec((None, 1, e), lambda bi, h, qi: (layer, 0, 0)),
        ],
        out_specs=pl.BlockSpec((None, tile, e), lambda bi, h, qi: (bi, qi, h)),
        out_shape=jax.ShapeDtypeStruct((b, t, w), BF16),
        compiler_params=_cparams("parallel", "parallel", "arbitrary"),
        name="attn_prompt",
    )(q, k, v, attn_lambda, subln_w)


def _attn_paged_kernel(pt_ref, q_ref, kn_ref, vn_ref, lp_ref, sw_ref, *refs,
                       pps, n_heads, head_dim, ts, lam_init):
    k_refs = refs[:pps]
    v_refs = refs[pps:2 * pps]
    o_ref = refs[2 * pps]
    qb_ref, m_ref, l_ref, acc_ref = refs[2 * pps + 1:]
    del pt_ref
    step = pl.program_id(1)
    e = 2 * head_dim
    w = n_heads * e
    rows = ts * 2 * n_heads
    grp = 2 * n_heads

    row_i = lax.broadcasted_iota(jnp.int32, (rows, w), 0)
    lane_i = lax.broadcasted_iota(jnp.int32, (rows, w), 1)
    row_t = row_i // grp
    row_c = (row_i % grp) // n_heads
    row_h = row_i % n_heads
    own_head = (lane_i // e) == row_h
    own_comp = own_head & (((lane_i % e) // head_dim) == row_c)

    @pl.when(step == 0)
    def _():
        qb = jnp.zeros((rows, w), F32)
        for t in range(ts):
            qb = jnp.where(own_comp & (row_t == t), q_ref[t:t + 1, :], qb)
        qb_ref[...] = qb
        m_ref[...] = jnp.full(m_ref.shape, -jnp.inf, F32)
        l_ref[...] = jnp.zeros(l_ref.shape, F32)
        acc_ref[...] = jnp.zeros(acc_ref.shape, F32)

    qb = qb_ref[...]
    qb16 = qb.astype(BF16)
    s = jnp.concatenate([_dot_nt(qb16, k_refs[r][...].astype(BF16)) for r in range(pps)], axis=1)
    m_old = m_ref[...]
    m_new = jnp.maximum(m_old, jnp.max(s, axis=-1, keepdims=True))
    alpha = jnp.exp(m_old - m_new)
    p = jnp.exp(s - m_new)
    l_new = alpha * l_ref[...] + jnp.sum(p, axis=-1, keepdims=True)
    acc = alpha * acc_ref[...]
    page = k_refs[0].shape[0]
    for r in range(pps):
        acc = acc + _dot(p[:, r * page:(r + 1) * page].astype(BF16), v_refs[r][...].astype(BF16))
    m_ref[...] = m_new
    l_ref[...] = l_new
    acc_ref[...] = acc

    @pl.when(step == pl.num_programs(1) - 1)
    def _():
        m = m_ref[...]
        l = l_ref[...]
        a = acc_ref[...]
        s_new = [jnp.sum(qb * kn_ref[t:t + 1, :], axis=-1, keepdims=True) for t in range(ts)]
        row_t1 = row_t[:, 0:1]
        s_new = [jnp.where(row_t1 >= t, s_new[t], -jnp.inf) for t in range(ts)]
        m_fin = m
        for t in range(ts):
            m_fin = jnp.maximum(m_fin, s_new[t])
        alpha_f = jnp.exp(m - m_fin)
        l = alpha_f * l
        a = alpha_f * a
        for t in range(ts):
            p_t = jnp.exp(s_new[t] - m_fin)
            l = l + p_t
            a = a + p_t * vn_ref[t:t + 1, :]
        d = a / l
        lam = _lambda_value(lp_ref[...], lam_init)
        o = d - lam * pltpu.roll(d, rows - n_heads, 0)
        o = jnp.where(own_head & (row_c == 0), o, 0.0)
        ms = jnp.sum(o * o, axis=-1, keepdims=True) * (1.0 / e)
        on = o * lax.rsqrt(ms + NORM_EPS) * sw_ref[...] * (1.0 - lam_init)
        o_ref[...] = jnp.sum(on.reshape(ts, grp, w), axis=1).astype(o_ref.dtype)


def _attn_paged(q, k_new, v_new, cache_k, cache_v, page_table, attn_lambda, subln_w_tiled, layer, lam_init,
                *, n_heads, head_dim):
    db, ts, w = q.shape
    page = cache_k.shape[2]
    n_pages = page_table.shape[1]
    pps = _pick_tile(n_pages, 8, 1)
    rows = ts * 2 * n_heads
    tok_spec = pl.BlockSpec((None, ts, w), lambda b, s, pt: (b, 0, 0))

    def page_spec(r):
        return pl.BlockSpec((None, None, page, w), lambda b, s, pt: (layer, pt[b, s * pps + r], 0, 0))

    grid_spec = pltpu.PrefetchScalarGridSpec(
        num_scalar_prefetch=1,
        grid=(db, n_pages // pps),
        in_specs=[tok_spec, tok_spec, tok_spec,
                  pl.BlockSpec((None, 4, head_dim), lambda b, s, pt: (layer, 0, 0)),
                  pl.BlockSpec((None, 1, w), lambda b, s, pt: (layer, 0, 0))]
                 + [page_spec(r) for r in range(pps)] + [page_spec(r) for r in range(pps)],
        out_specs=pl.BlockSpec((None, ts, w), lambda b, s, pt: (b, 0, 0)),
        scratch_shapes=[pltpu.VMEM((rows, w), F32), pltpu.VMEM((rows, 1), F32),
                        pltpu.VMEM((rows, 1), F32), pltpu.VMEM((rows, w), F32)],
    )
    return pl.pallas_call(
        functools.partial(_attn_paged_kernel, pps=pps, n_heads=n_heads, head_dim=head_dim, ts=ts,
                          lam_init=lam_init),
        grid_spec=grid_spec,
        out_shape=jax.ShapeDtypeStruct((db, ts, w), BF16),
        compiler_params=_cparams("parallel", "arbitrary"),
        name="attn_paged",
    )(page_table, q, k_new, v_new, attn_lambda, subln_w_tiled,
      *([cache_k] * pps), *([cache_v] * pps))


def _gelu_tanh(x):
    return 0.5 * x * (1.0 + jnp.tanh(math.sqrt(2.0 / math.pi) * (x + 0.044715 * (x * x * x))))


def _conv_from_buffer(buf_ref, w_ref, *, halo, width, rows, row_block):
    first = halo - (width - 1)
    outs = []
    for r0 in range(0, rows, row_block):
        rb = min(row_block, rows - r0)
        acc = None
        for phase in range(min(SUBLANES, width)):
            taps = list(range(phase, width, SUBLANES))
            span = rb + (taps[-1] - phase)
            zs = buf_ref[pl.ds(first + r0 + phase, span), :]
            for j in taps:
                term = w_ref[j:j + 1, :] * zs[j - phase:j - phase + rb, :]
                acc = term if acc is None else acc + term
        outs.append(acc)
    return outs[0] if len(outs) == 1 else jnp.concatenate(outs, axis=0)


def _mixers_kernel(xb_ref, gb_ref, ca_ref, cb_ref, sb_ref, sc_ref, sx_ref,
                   h0_ref, lru_st_ref, cm_st_ref, sc_st_ref,
                   lcw_ref, lcb_ref, gw_ref, gbias_ref, ap_ref,
                   ccw_ref, ccb_ref, lnw_ref, lnb_ref, scw_ref,
                   y_ref, h_out_ref, lru_out_ref, cm_out_ref, sc_out_ref,
                   lru_buf, cm_buf, sc_buf, h_ref, hs_buf,
                   *, tt, n_t, halos, widths, blk):
    ti = pl.program_id(1)
    w = xb_ref.shape[-1]
    (halo_l, halo_c, halo_s), (wid_l, wid_c, wid_s) = halos, widths

    @pl.when(ti == 0)
    def _():
        lru_buf[0:halo_l, :] = lru_st_ref[...]
        cm_buf[0:halo_c, :] = cm_st_ref[...]
        sc_buf[0:halo_s, :] = sc_st_ref[...]
        h_ref[...] = h0_ref[...]

    lru_buf[halo_l:halo_l + tt, :] = xb_ref[...]
    xc = _conv_from_buffer(lru_buf, lcw_ref, halo=halo_l, width=wid_l, rows=tt, row_block=tt) + lcb_ref[...]
    xc16 = xc.astype(BF16)
    gates = []
    for g in range(2):
        parts = [_dot(xc16[:, n * blk:(n + 1) * blk], gw_ref[g, n].astype(BF16)) for n in range(w // blk)]
        gates.append(jnp.concatenate(parts, axis=1) + gbias_ref[g:g + 1, :])
    r_gate = jax.nn.sigmoid(gates[0])
    i_gate = jax.nn.sigmoid(gates[1])
    log_a = -LRU_C * r_gate * jax.nn.softplus(-ap_ref[...])
    a = jnp.exp(log_a)
    u = jnp.sqrt(1.0 - jnp.exp(2.0 * log_a)) * (i_gate * xc)
    if tt % SUBLANES == 0:
        row = lax.broadcasted_iota(jnp.int32, (tt, w), 0)
        dist = 1
        while dist < tt:
            live = row >= dist
            a_prev = jnp.where(live, pltpu.roll(a, dist, 0), 1.0)
            u_prev = jnp.where(live, pltpu.roll(u, dist, 0), 0.0)
            u = a * u_prev + u
            a = a * a_prev
            dist *= 2
        hs = u + a * h_ref[...]
        h_last = hs[tt - 1:tt, :]
    else:
        h_last = h_ref[...]
        for r in range(tt):
            h_last = a[r:r + 1, :] * h_last + u[r:r + 1, :]
            hs_buf[r:r + 1, :] = h_last
        hs = hs_buf[0:tt, :]
    h_ref[...] = h_last
    y_ref[:, 0:w] = (_gelu_tanh(gb_ref[...]) * hs).astype(y_ref.dtype)

    cm_buf[halo_c:halo_c + tt, :] = ca_ref[...] * jax.nn.sigmoid(cb_ref[...])
    uc = _conv_from_buffer(cm_buf, ccw_ref, halo=halo_c, width=wid_c, rows=tt, row_block=32) + ccb_ref[...]
    mu = jnp.mean(uc, axis=-1, keepdims=True)
    cen = uc - mu
    var = jnp.mean(cen * cen, axis=-1, keepdims=True)
    ln = cen * lax.rsqrt(var + NORM_EPS) * lnw_ref[...] + lnb_ref[...]
    y_ref[:, w:2 * w] = (ln * jax.nn.sigmoid(ln)).astype(y_ref.dtype)

    sc_buf[halo_s:halo_s + tt, :] = sc_ref[...] * sx_ref[...]
    sconv = _conv_from_buffer(sc_buf, scw_ref, halo=halo_s, width=wid_s, rows=tt, row_block=tt)
    y_ref[:, 2 * w:3 * w] = (sb_ref[...] * sconv).astype(y_ref.dtype)

    @pl.when(ti == n_t - 1)
    def _():
        h_out_ref[...] = h_last
        lru_out_ref[...] = lru_buf[pl.ds(tt + halo_l - (wid_l - 1), wid_l - 1), :]
        cm_out_ref[...] = cm_buf[pl.ds(tt + halo_c - (wid_c - 1), wid_c - 1), :]
        sc_out_ref[...] = sc_buf[pl.ds(tt + halo_s - (wid_s - 1), wid_s - 1), :]

    if n_t > 1:
        @pl.when(ti < n_t - 1)
        def _():
            lru_buf[0:halo_l, :] = lru_buf[tt:tt + halo_l, :]
            cm_buf[0:halo_c, :] = cm_buf[tt:tt + halo_c, :]
            sc_buf[0:halo_s, :] = sc_buf[tt:tt + halo_s, :]


def _pad_state(st, halo):
    return jnp.pad(st, ((0, 0), (halo - st.shape[1], 0), (0, 0)))


def _mixers(z, states, P, layer, *, w):
    b, t, _ = z.shape
    h0, lru_st, cm_st, sc_st = states
    widths = (P['lru_conv_w'].shape[1], P['cm_conv_w'].shape[1], P['sc_conv_w'].shape[1])
    halos = tuple(-(-(wd - 1) // SUBLANES) * SUBLANES for wd in widths)
    tt = _pick_tile(t, 256, SUBLANES)
    n_t = t // tt
    nb = P['lru_gate_w'].shape[2]
    blk = w // nb
    zspec = lambda slot: pl.BlockSpec((None, tt, w), lambda bi, ti: (bi, ti, slot))
    bspec = lambda rows: pl.BlockSpec((None, rows, w), lambda bi, ti: (bi, 0, 0))
    lspec = lambda rows: pl.BlockSpec((None, rows, w), lambda bi, ti: (layer, 0, 0))
    in_specs = [zspec(s) for s in range(3, 10)]
    in_specs += [bspec(1), bspec(halos[0]), bspec(halos[1]), bspec(halos[2])]
    in_specs += [lspec(widths[0]), lspec(1),
                 pl.BlockSpec((None, 2, nb, blk, blk), lambda bi, ti: (layer, 0, 0, 0, 0)),
                 lspec(2), lspec(1), lspec(widths[1]), lspec(1), lspec(1), lspec(1), lspec(widths[2])]
    out_specs = [pl.BlockSpec((None, tt, 3 * w), lambda bi, ti: (bi, ti, 0)),
                 bspec(1), bspec(widths[0] - 1), bspec(widths[1] - 1), bspec(widths[2] - 1)]
    out_shape = [jax.ShapeDtypeStruct((b, t, 3 * w), BF16),
                 jax.ShapeDtypeStruct((b, 1, w), F32),
                 jax.ShapeDtypeStruct((b, widths[0] - 1, w), F32),
                 jax.ShapeDtypeStruct((b, widths[1] - 1, w), F32),
                 jax.ShapeDtypeStruct((b, widths[2] - 1, w), F32)]
    row3 = lambda a: a.reshape(a.shape[0], 1, a.shape[1])
    return pl.pallas_call(
        functools.partial(_mixers_kernel, tt=tt, n_t=n_t, halos=halos, widths=widths, blk=blk),
        grid=(b, n_t),
        in_specs=in_specs,
        out_specs=out_specs,
        out_shape=out_shape,
        scratch_shapes=[pltpu.VMEM((halos[0] + tt, w), F32), pltpu.VMEM((halos[1] + tt, w), F32),
                        pltpu.VMEM((halos[2] + tt, w), F32), pltpu.VMEM((1, w), F32),
                        pltpu.VMEM((SUBLANES, w), F32)],
        compiler_params=_cparams("parallel", "arbitrary"),
        name="mixers",
    )(*([z] * 7), h0, _pad_state(lru_st, halos[0]), _pad_state(cm_st, halos[1]), _pad_state(sc_st, halos[2]),
      P['lru_conv_w'], row3(P['lru_conv_b']), P['lru_gate_w'], P['lru_gate_b'], row3(P['lru_a_param']),
      P['cm_conv_w'], row3(P['cm_conv_b']), row3(P['cm_ln_w']), row3(P['cm_ln_b']), P['sc_conv_w'])


def _layer(x, pos_tables, states, P, layer, lam_init, dims, paged=None):
    b, t, d = x.shape
    w, n_heads, head_dim = dims
    nw = P['norm_w']
    x2 = x.reshape(b * t, d)
    h = _norm_matmul(x2, nw, P['ffn1_w_gu'], layer, 0, swiglu=True, out_dtype=BF16)
    x2 = _matmul_norm_res([h], P['ffn1_w_down'], x2, nw, layer, 1, 0.5)
    z = _norm_matmul(x2, nw, P['w_in'], layer, 2, swiglu=False, out_dtype=F32).reshape(b, t, N_IN_SLOTS * w)
    cos, sin = pos_tables
    if paged is None:
        q, k_rot, k16, v16 = _rope(z, cos, sin, w=w, head_dim=head_dim, q_dtype=BF16)
        ya = _attn_prompt(q, k16, v16, P['attn_lambda'], P['attn_subln_w'], layer, lam_init,
                          n_heads=n_heads, head_dim=head_dim)
    else:
        cache_k, cache_v, page_table = paged
        q, k_rot, _, _ = _rope(z, cos, sin, w=w, head_dim=head_dim, q_dtype=F32)
        ya = _attn_paged(q, k_rot, z[:, :, 2 * w:3 * w], cache_k, cache_v, page_table, P['attn_lambda'],
                         P['attn_subln_w_tiled'], layer, lam_init, n_heads=n_heads, head_dim=head_dim)
    v_rows = z[:, :, 2 * w:3 * w]
    ybcd, h_t, lru_buf, cm_buf, sc_buf = _mixers(z, states, P, layer, w=w)
    x2 = _matmul_norm_res([ya.reshape(b * t, w), ybcd.reshape(b * t, 3 * w)], P['w_out'], x2, nw, layer, 3, 1.0)
    h = _norm_matmul(x2, nw, P['ffn2_w_gu'], layer, 4, swiglu=True, out_dtype=BF16)
    x2 = _matmul_norm_res([h], P['ffn2_w_down'], x2, nw, layer, 5, 0.5)
    e = 2 * head_dim
    outs = (k_rot.reshape(b, t, n_heads, e), v_rows.reshape(b, t, n_heads, e), h_t.reshape(b, w),
            lru_buf, cm_buf, sc_buf)
    return x2.reshape(b, t, d), outs


def kernel(x_prompt, x_sample, cache_k, cache_v, state_lru_h, state_lru_conv, state_cm_conv, state_sc_conv, page_table, norm_w, ffn1_w_gu, ffn1_w_down, ffn2_w_gu, ffn2_w_down, w_in, w_out, attn_lambda, attn_subln_w, lru_conv_w, lru_conv_b, lru_gate_w, lru_gate_b, lru_a_param, cm_conv_w, cm_conv_b, cm_ln_w, cm_ln_b, sc_conv_w):
    b, t, d = x_prompt.shape
    db, ts, _ = x_sample.shape
    depth = norm_w.shape[0]
    w = d // N_GROUPS
    n_heads = cache_k.shape[3]
    e = cache_k.shape[4]
    head_dim = e // 2
    assert n_heads * e == w and w_in.shape[2] == N_IN_SLOTS * w
    n_pool, page = cache_k.shape[1], cache_k.shape[2]
    past = page_table.shape[1] * page
    dims = (w, n_heads, head_dim)

    P = {
        'norm_w': norm_w,
        'ffn1_w_gu': ffn1_w_gu.astype(BF16), 'ffn1_w_down': ffn1_w_down.astype(BF16),
        'ffn2_w_gu': ffn2_w_gu.astype(BF16), 'ffn2_w_down': ffn2_w_down.astype(BF16),
        'w_in': w_in.astype(BF16), 'w_out': w_out.astype(BF16),
        'attn_lambda': attn_lambda,
        'attn_subln_w': attn_subln_w.reshape(depth, 1, e),
        'attn_subln_w_tiled': jnp.tile(attn_subln_w, (1, n_heads)).reshape(depth, 1, w),
        'lru_conv_w': lru_conv_w, 'lru_conv_b': lru_conv_b, 'lru_gate_w': lru_gate_w,
        'lru_gate_b': lru_gate_b, 'lru_a_param': lru_a_param,
        'cm_conv_w': cm_conv_w, 'cm_conv_b': cm_conv_b, 'cm_ln_w': cm_ln_w, 'cm_ln_b': cm_ln_b,
        'sc_conv_w': sc_conv_w,
    }
    cache_k4 = cache_k.reshape(depth, n_pool, page, w)
    cache_v4 = cache_v.reshape(depth, n_pool, page, w)
    tables_p = _rope_tables(jnp.arange(t), w, head_dim)
    tables_s = _rope_tables(past + jnp.arange(ts), w, head_dim)
    zeros_p = (jnp.zeros((b, 1, w), F32), jnp.zeros((b, lru_conv_w.shape[1] - 1, w), F32),
               jnp.zeros((b, cm_conv_w.shape[1] - 1, w), F32), jnp.zeros((b, sc_conv_w.shape[1] - 1, w), F32))

    xp, xs = x_prompt, x_sample
    outs_p, outs_s = [], []
    for l in range(depth):
        lam_init = 0.8 - 0.6 * math.exp(-0.3 * l)
        xp, sp = _layer(xp, tables_p, zeros_p, P, l, lam_init, dims)
        outs_p.append(sp)
        st_s = (state_lru_h[l].reshape(db, 1, w), state_lru_conv[l], state_cm_conv[l], state_sc_conv[l])
        xs, ss = _layer(xs, tables_s, st_s, P, l, lam_init, dims, paged=(cache_k4, cache_v4, page_table))
        outs_s.append(ss)
    stack = lambda outs, i: jnp.stack([o[i] for o in outs])
    return (xp, xs) + tuple(stack(outs_p, i) for i in range(6)) + tuple(stack(outs_s, i) for i in range(6))
```

```python
import functools
import math

import jax
import jax.numpy as jnp
from jax import lax
from jax.experimental import pallas as pl
from jax.experimental.pallas import tpu as pltpu

F32 = jnp.float32
BF16 = jnp.bfloat16

NORM_EPS = 1e-6
ROPE_THETA = 10000.0
LRU_C = 8.0
N_GROUPS = 4
N_IN_SLOTS = 10
VMEM_LIMIT_BYTES = 56 * 1024 * 1024
LANES = 128
SUBLANES = 8


def _cparams(*sem):
    return pltpu.CompilerParams(dimension_semantics=sem, vmem_limit_bytes=VMEM_LIMIT_BYTES)


def _pick_tile(n, target, quantum):
    if n <= target:
        return n
    t = (target // quantum) * quantum
    while t >= quantum:
        if n % t == 0:
            return t
        t -= quantum
    return n


def _dot(a, b):
    return jnp.dot(a, b, preferred_element_type=F32)


def _dot_nt(a, b):
    return lax.dot_general(a, b, (((1,), (1,)), ((), ())), preferred_element_type=F32)


def _rms_rows(x, w_row):
    ms = jnp.mean(x * x, axis=-1, keepdims=True)
    return x * lax.rsqrt(ms + NORM_EPS) * w_row


def _norm_matmul_kernel(x_ref, nw_ref, *refs, norm_row, swiglu):
    if swiglu:
        wg_ref, wu_ref, out_ref, xn_ref = refs
    else:
        w_ref, out_ref, xn_ref = refs

    @pl.when(pl.program_id(1) == 0)
    def _():
        xn_ref[...] = _rms_rows(x_ref[...], nw_ref[norm_row:norm_row + 1, :]).astype(BF16)

    xn = xn_ref[...]
    if swiglu:
        g = _dot(xn, wg_ref[...])
        u = _dot(xn, wu_ref[...])
        out_ref[...] = (g * jax.nn.sigmoid(g) * u).astype(out_ref.dtype)
    else:
        out_ref[...] = _dot(xn, w_ref[...]).astype(out_ref.dtype)


def _norm_matmul(x, norm_w, w, layer, norm_row, *, swiglu, out_dtype):
    m, d = x.shape
    n = w.shape[2]
    n_out = n // 2 if swiglu else n
    tm = _pick_tile(m, 1024, 16)
    tn = _pick_tile(n_out, 512, LANES)
    n_blocks = n_out // tn
    in_specs = [
        pl.BlockSpec((tm, d), lambda i, j: (i, 0)),
        pl.BlockSpec((None, norm_w.shape[1], d), lambda i, j: (layer, 0, 0)),
        pl.BlockSpec((None, d, tn), lambda i, j: (layer, 0, j)),
    ]
    args = [x, norm_w, w]
    if swiglu:
        in_specs.append(pl.BlockSpec((None, d, tn), lambda i, j: (layer, 0, j + n_blocks)))
        args.append(w)
    return pl.pallas_call(
        functools.partial(_norm_matmul_kernel, norm_row=norm_row, swiglu=swiglu),
        grid=(m // tm, n_blocks),
        in_specs=in_specs,
        out_specs=pl.BlockSpec((tm, tn), lambda i, j: (i, j)),
        out_shape=jax.ShapeDtypeStruct((m, n_out), out_dtype),
        scratch_shapes=[pltpu.VMEM((tm, d), BF16)],
        compiler_params=_cparams("parallel", "arbitrary"),
        name="norm_matmul_swiglu" if swiglu else "norm_matmul",
    )(*args)


def _matmul_norm_res_kernel(*refs, n_parts, part_rows, norm_row, scale):
    h_refs = refs[:n_parts]
    w_ref, x_ref, nw_ref, out_ref = refs[n_parts:]
    y = None
    row = 0
    for h_ref, rows in zip(h_refs, part_rows):
        part = _dot(h_ref[...], w_ref[row:row + rows, :])
        y = part if y is None else y + part
        row += rows
    yn = _rms_rows(y, nw_ref[norm_row:norm_row + 1, :])
    out_ref[...] = x_ref[...] + scale * yn


def _matmul_norm_res(h_parts, w, x, norm_w, layer, norm_row, scale):
    m, d = x.shape
    k = w.shape[1]
    part_rows = tuple(h.shape[1] for h in h_parts)
    assert sum(part_rows) == k
    tm = _pick_tile(m, 256 if k > 4096 else 512, 16)
    in_specs = [pl.BlockSpec((tm, r), lambda i: (i, 0)) for r in part_rows]
    in_specs += [
        pl.BlockSpec((None, k, d), lambda i: (layer, 0, 0), pipeline_mode=pl.Buffered(1)),
        pl.BlockSpec((tm, d), lambda i: (i, 0)),
        pl.BlockSpec((None, norm_w.shape[1], d), lambda i: (layer, 0, 0)),
    ]
    return pl.pallas_call(
        functools.partial(_matmul_norm_res_kernel, n_parts=len(h_parts), part_rows=part_rows,
                          norm_row=norm_row, scale=scale),
        grid=(m // tm,),
        in_specs=in_specs,
        out_specs=pl.BlockSpec((tm, d), lambda i: (i, 0)),
        out_shape=jax.ShapeDtypeStruct((m, d), F32),
        compiler_params=_cparams("parallel"),
        name="matmul_norm_res",
    )(*h_parts, w, x, norm_w)


def _swap_halves(x, half):
    n = x.shape[-1]
    lane = lax.broadcasted_iota(jnp.int32, x.shape, x.ndim - 1)
    first = (lane & half) == 0
    return jnp.where(first, pltpu.roll(x, n - half, x.ndim - 1), pltpu.roll(x, half, x.ndim - 1))


def _rope_kernel(q_ref, k_ref, v_ref, cos_ref, sin_ref, qo_ref, ko_ref, kb_ref, vb_ref, *, half, q_scale):
    cos = cos_ref[...]
    sin = sin_ref[...]
    q = q_ref[...]
    k = k_ref[...]
    qr = q * cos + _swap_halves(q, half) * sin
    kr = k * cos + _swap_halves(k, half) * sin
    qo_ref[...] = (qr * q_scale).astype(qo_ref.dtype)
    ko_ref[...] = kr
    kb_ref[...] = kr.astype(BF16)
    vb_ref[...] = v_ref[...].astype(BF16)


def _rope(z, cos, sin, *, w, head_dim, q_dtype):
    b, t, _ = z.shape
    tt = _pick_tile(t, 512, 16)
    zspec = lambda slot: pl.BlockSpec((None, tt, w), lambda bi, ti: (bi, ti, slot))
    tspec = pl.BlockSpec((tt, w), lambda bi, ti: (ti, 0))
    ospec = pl.BlockSpec((None, tt, w), lambda bi, ti: (bi, ti, 0))
    return pl.pallas_call(
        functools.partial(_rope_kernel, half=head_dim // 2, q_scale=head_dim ** -0.5),
        grid=(b, t // tt),
        in_specs=[zspec(0), zspec(1), zspec(2), tspec, tspec],
        out_specs=[ospec, ospec, ospec, ospec],
        out_shape=[jax.ShapeDtypeStruct((b, t, w), q_dtype), jax.ShapeDtypeStruct((b, t, w), F32),
                   jax.ShapeDtypeStruct((b, t, w), BF16), jax.ShapeDtypeStruct((b, t, w), BF16)],
        compiler_params=_cparams("parallel", "parallel"),
        name="rope",
    )(z, z, z, cos, sin)


def _rope_tables(pos, w, head_dim):
    half = head_dim // 2
    inv = ROPE_THETA ** (-jnp.arange(half, dtype=F32) / half)
    ang = pos.astype(F32)[:, None] * inv[None, :]
    cos = jnp.cos(ang)
    sin = jnp.sin(ang)
    cos_d = jnp.concatenate([cos, cos], axis=-1)
    sin_d = jnp.concatenate([-sin, sin], axis=-1)
    reps = w // head_dim
    return jnp.tile(cos_d, (1, reps)), jnp.tile(sin_d, (1, reps))


def _lambda_value(lp, lam_init):
    s1 = jnp.sum(lp[0:1, :] * lp[1:2, :], axis=-1, keepdims=True)
    s2 = jnp.sum(lp[2:3, :] * lp[3:4, :], axis=-1, keepdims=True)
    return jnp.exp(s1) - jnp.exp(s2) + lam_init


def _attn_prompt_kernel(q_ref, k_ref, v_ref, lp_ref, sw_ref, o_ref, *, tile, head_dim, lam_init):
    t, e = q_ref.shape
    lam = _lambda_value(lp_ref[...], lam_init)
    lane = lax.broadcasted_iota(jnp.int32, (tile, e), 1)
    row = lax.broadcasted_iota(jnp.int32, (tile, tile), 0)
    col = lax.broadcasted_iota(jnp.int32, (tile, tile), 1)
    keep = col <= row
    for i in range(t // tile):
        kv = (i + 1) * tile
        q = q_ref[i * tile:kv, :]
        ks = k_ref[0:kv, :]
        vx = jnp.concatenate([v_ref[0:kv, :], jnp.ones((kv, e), BF16)], axis=1)
        normed = []
        for c in range(2):
            qc = jnp.where((lane >= c * head_dim) & (lane < (c + 1) * head_dim), q, jnp.zeros_like(q))
            s = _dot_nt(qc, ks)
            s_diag = jnp.where(keep, s[:, i * tile:], -jnp.inf)
            m = jnp.max(s_diag, axis=-1, keepdims=True)
            if i > 0:
                s_full = s[:, :i * tile]
                m = jnp.maximum(m, jnp.max(s_full, axis=-1, keepdims=True))
                p = jnp.concatenate([jnp.exp(s_full - m), jnp.exp(s_diag - m)], axis=1)
            else:
                p = jnp.exp(s_diag - m)
            acc = _dot(p.astype(BF16), vx)
            normed.append(acc[:, :e] / acc[:, e:])
        o = normed[0] - lam * normed[1]
        o_ref[i * tile:kv, :] = (_rms_rows(o, sw_ref[...]) * (1.0 - lam_init)).astype(o_ref.dtype)


def _attn_prompt(q, k, v, attn_lambda, subln_w, layer, lam_init, *, n_heads, head_dim):
    b, t, w = q.shape
    e = 2 * head_dim
    tile = _pick_tile(t, 256, LANES)
    head_spec = pl.BlockSpec((None, t, e), lambda bi, h: (bi, 0, h))
    return pl.pallas_call(
        functools.partial(_attn_prompt_kernel, tile=tile, head_dim=head_dim, lam_init=lam_init),
        grid=(b, n_heads),
        in_specs=[
            head_spec, head_spec, head_spec,
            pl.BlockSpec((None, 4, head_dim), lambda bi, h: (layer, 0, 0)),
            pl.BlockSpec((None, 1, e), lambda bi, h: (layer, 0, 0)),
        ],
        out_specs=head_spec,
        out_shape=jax.ShapeDtypeStruct((b, t, w), BF16),
        compiler_params=_cparams("parallel", "parallel"),
        name="attn_prompt",
    )(q, k, v, attn_lambda, subln_w)


def _attn_paged_kernel(pt_ref, q_ref, kn_ref, vn_ref, lp_ref, sw_ref, *refs,
                       pps, n_heads, head_dim, ts, lam_init):
    k_refs = refs[:pps]
    v_refs = refs[pps:2 * pps]
    o_ref = refs[2 * pps]
    qb_ref, m_ref, l_ref, acc_ref = refs[2 * pps + 1:]
    del pt_ref
    step = pl.program_id(1)
    e = 2 * head_dim
    half = n_heads * ts
    rows = 2 * half
    page_rows = k_refs[0].shape[0]

    row_i = lax.broadcasted_iota(jnp.int32, (rows, e), 0)
    lane_i = lax.broadcasted_iota(jnp.int32, (rows, e), 1)
    row_c = row_i // half
    row_h = (row_i % half) // ts
    row_t = row_i % ts

    @pl.when(step == 0)
    def _():
        own_comp = (lane_i // head_dim) == row_c
        qb = jnp.zeros((rows, e), F32)
        for h in range(n_heads):
            for t in range(ts):
                qb = jnp.where(own_comp & (row_h == h) & (row_t == t), q_ref[t:t + 1, h * e:(h + 1) * e], qb)
        qb_ref[...] = qb
        m_ref[...] = jnp.full(m_ref.shape, -jnp.inf, F32)
        l_ref[...] = jnp.zeros(l_ref.shape, F32)
        acc_ref[...] = jnp.zeros(acc_ref.shape, F32)

    qb = qb_ref[...]
    qb16 = qb.astype(BF16)
    col_h = lax.broadcasted_iota(jnp.int32, (rows, page_rows), 1) % n_heads
    own_cols = col_h == (lax.broadcasted_iota(jnp.int32, (rows, page_rows), 0) % half) // ts
    s = [jnp.where(own_cols, _dot_nt(qb16, k_refs[r][...].astype(BF16)), -jnp.inf) for r in range(pps)]
    m_old = m_ref[...]
    m_new = m_old
    for r in range(pps):
        m_new = jnp.maximum(m_new, jnp.max(s[r], axis=-1, keepdims=True))
    alpha = jnp.exp(m_old - m_new)
    l_new = alpha * l_ref[...]
    acc = alpha * acc_ref[...]
    for r in range(pps):
        p = jnp.exp(s[r] - m_new)
        l_new = l_new + jnp.sum(p, axis=-1, keepdims=True)
        acc = acc + _dot(p.astype(BF16), v_refs[r][...].astype(BF16))
    m_ref[...] = m_new
    l_ref[...] = l_new
    acc_ref[...] = acc

    @pl.when(step == pl.num_programs(1) - 1)
    def _():
        m = m_ref[...]
        l = l_ref[...]
        a = acc_ref[...]

        def own_head_rows(ref, t):
            out = jnp.zeros((rows, e), F32)
            for h in range(n_heads):
                out = jnp.where(row_h == h, ref[t:t + 1, h * e:(h + 1) * e], out)
            return out

        row_t1 = row_t[:, 0:1]
        s_new = []
        for t in range(ts):
            s_t = jnp.sum(qb * own_head_rows(kn_ref, t), axis=-1, keepdims=True)
            s_new.append(jnp.where(row_t1 >= t, s_t, -jnp.inf))
        m_fin = m
        for t in range(ts):
            m_fin = jnp.maximum(m_fin, s_new[t])
        alpha_f = jnp.exp(m - m_fin)
        l = alpha_f * l
        a = alpha_f * a
        for t in range(ts):
            p_t = jnp.exp(s_new[t] - m_fin)
            l = l + p_t
            a = a + p_t * own_head_rows(vn_ref, t)
        d = a / l
        lam = _lambda_value(lp_ref[...], lam_init)
        o = d[0:half, :] - lam * d[half:rows, :]
        on = _rms_rows(o, sw_ref[...]) * (1.0 - lam_init)
        for h in range(n_heads):
            o_ref[:, h * e:(h + 1) * e] = on[h * ts:(h + 1) * ts, :].astype(o_ref.dtype)


def _attn_paged(q, k_new, v_new, cache_k, cache_v, page_table, attn_lambda, subln_w, layer, lam_init,
                *, n_heads, head_dim):
    db, ts, w = q.shape
    e = 2 * head_dim
    page_rows = cache_k.shape[2]
    n_pages = page_table.shape[1]
    pps = _pick_tile(n_pages, 8, 1)
    rows = 2 * n_heads * ts
    tok_spec = pl.BlockSpec((None, ts, w), lambda b, s, pt: (b, 0, 0))

    def page_spec(r):
        return pl.BlockSpec((None, None, page_rows, e), lambda b, s, pt: (layer, pt[b, s * pps + r], 0, 0))

    grid_spec = pltpu.PrefetchScalarGridSpec(
        num_scalar_prefetch=1,
        grid=(db, n_pages // pps),
        in_specs=[tok_spec, tok_spec, tok_spec,
                  pl.BlockSpec((None, 4, head_dim), lambda b, s, pt: (layer, 0, 0)),
                  pl.BlockSpec((None, 1, e), lambda b, s, pt: (layer, 0, 0))]
                 + [page_spec(r) for r in range(pps)] + [page_spec(r) for r in range(pps)],
        out_specs=pl.BlockSpec((None, ts, w), lambda b, s, pt: (b, 0, 0)),
        scratch_shapes=[pltpu.VMEM((rows, e), F32), pltpu.VMEM((rows, 1), F32),
                        pltpu.VMEM((rows, 1), F32), pltpu.VMEM((rows, e), F32)],
    )
    return pl.pallas_call(
        functools.partial(_attn_paged_kernel, pps=pps, n_heads=n_heads, head_dim=head_dim, ts=ts,
                          lam_init=lam_init),
        grid_spec=grid_spec,
        out_shape=jax.ShapeDtypeStruct((db, ts, w), BF16),
        compiler_params=_cparams("parallel", "arbitrary"),
        name="attn_paged",
    )(page_table, q, k_new, v_new, attn_lambda, subln_w,
      *([cache_k] * pps), *([cache_v] * pps))


def _gelu_tanh(x):
    return 0.5 * x * (1.0 + jnp.tanh(math.sqrt(2.0 / math.pi) * (x + 0.044715 * (x * x * x))))


def _conv_from_buffer(buf_ref, w_ref, *, halo, width, rows, row_block):
    first = halo - (width - 1)
    outs = []
    for r0 in range(0, rows, row_block):
        rb = min(row_block, rows - r0)
        acc = None
        for phase in range(min(SUBLANES, width)):
            taps = list(range(phase, width, SUBLANES))
            span = rb + (taps[-1] - phase)
            zs = buf_ref[pl.ds(first + r0 + phase, span), :]
            for j in taps:
                term = w_ref[j:j + 1, :] * zs[j - phase:j - phase + rb, :]
                acc = term if acc is None else acc + term
        outs.append(acc)
    return outs[0] if len(outs) == 1 else jnp.concatenate(outs, axis=0)


def _mixers_kernel(xb_ref, gb_ref, ca_ref, cb_ref, sb_ref, sc_ref, sx_ref,
                   h0_ref, lru_st_ref, cm_st_ref, sc_st_ref,
                   lcw_ref, lcb_ref, gw_ref, gbias_ref, ap_ref,
                   ccw_ref, ccb_ref, lnw_ref, lnb_ref, scw_ref,
                   y_ref, h_out_ref, lru_out_ref, cm_out_ref, sc_out_ref,
                   lru_buf, cm_buf, sc_buf, h_ref, hs_buf,
                   *, tt, n_t, halos, widths, blk):
    ti = pl.program_id(1)
    w = xb_ref.shape[-1]
    (halo_l, halo_c, halo_s), (wid_l, wid_c, wid_s) = halos, widths

    @pl.when(ti == 0)
    def _():
        lru_buf[0:halo_l, :] = lru_st_ref[...]
        cm_buf[0:halo_c, :] = cm_st_ref[...]
        sc_buf[0:halo_s, :] = sc_st_ref[...]
        h_ref[...] = h0_ref[...]

    lru_buf[halo_l:halo_l + tt, :] = xb_ref[...]
    xc = _conv_from_buffer(lru_buf, lcw_ref, halo=halo_l, width=wid_l, rows=tt, row_block=tt) + lcb_ref[...]
    xc16 = xc.astype(BF16)
    gates = []
    for g in range(2):
        parts = [_dot(xc16[:, n * blk:(n + 1) * blk], gw_ref[g, n].astype(BF16)) for n in range(w // blk)]
        gates.append(jnp.concatenate(parts, axis=1) + gbias_ref[g:g + 1, :])
    r_gate = jax.nn.sigmoid(gates[0])
    i_gate = jax.nn.sigmoid(gates[1])
    log_a = -LRU_C * r_gate * jax.nn.softplus(-ap_ref[...])
    a = jnp.exp(log_a)
    u = jnp.sqrt(1.0 - jnp.exp(2.0 * log_a)) * (i_gate * xc)
    if tt % SUBLANES == 0:
        row = lax.broadcasted_iota(jnp.int32, (tt, w), 0)
        dist = 1
        while dist < tt:
            live = row >= dist
            a_prev = jnp.where(live, pltpu.roll(a, dist, 0), 1.0)
            u_prev = jnp.where(live, pltpu.roll(u, dist, 0), 0.0)
            u = a * u_prev + u
            a = a * a_prev
            dist *= 2
        hs = u + a * h_ref[...]
        h_last = hs[tt - 1:tt, :]
    else:
        h_last = h_ref[...]
        for r in range(tt):
            h_last = a[r:r + 1, :] * h_last + u[r:r + 1, :]
            hs_buf[r:r + 1, :] = h_last
        hs = hs_buf[0:tt, :]
    h_ref[...] = h_last
    y_ref[:, 0:w] = (_gelu_tanh(gb_ref[...]) * hs).astype(y_ref.dtype)

    cm_buf[halo_c:halo_c + tt, :] = ca_ref[...] * jax.nn.sigmoid(cb_ref[...])
    uc = _conv_from_buffer(cm_buf, ccw_ref, halo=halo_c, width=wid_c, rows=tt, row_block=32) + ccb_ref[...]
    mu = jnp.mean(uc, axis=-1, keepdims=True)
    cen = uc - mu
    var = jnp.mean(cen * cen, axis=-1, keepdims=True)
    ln = cen * lax.rsqrt(var + NORM_EPS) * lnw_ref[...] + lnb_ref[...]
    y_ref[:, w:2 * w] = (ln * jax.nn.sigmoid(ln)).astype(y_ref.dtype)

    sc_buf[halo_s:halo_s + tt, :] = sc_ref[...] * sx_ref[...]
    sconv = _conv_from_buffer(sc_buf, scw_ref, halo=halo_s, width=wid_s, rows=tt, row_block=tt)
    y_ref[:, 2 * w:3 * w] = (sb_ref[...] * sconv).astype(y_ref.dtype)

    @pl.when(ti == n_t - 1)
    def _():
        h_out_ref[...] = h_last
        lru_out_ref[...] = lru_buf[pl.ds(tt + halo_l - (wid_l - 1), wid_l - 1), :]
        cm_out_ref[...] = cm_buf[pl.ds(tt + halo_c - (wid_c - 1), wid_c - 1), :]
        sc_out_ref[...] = sc_buf[pl.ds(tt + halo_s - (wid_s - 1), wid_s - 1), :]

    if n_t > 1:
        @pl.when(ti < n_t - 1)
        def _():
            lru_buf[0:halo_l, :] = lru_buf[tt:tt + halo_l, :]
            cm_buf[0:halo_c, :] = cm_buf[tt:tt + halo_c, :]
            sc_buf[0:halo_s, :] = sc_buf[tt:tt + halo_s, :]


def _pad_state(st, halo):
    return jnp.pad(st, ((0, 0), (halo - st.shape[1], 0), (0, 0)))


def _mixers(z, states, P, layer, *, w):
    b, t, _ = z.shape
    h0, lru_st, cm_st, sc_st = states
    widths = (P['lru_conv_w'].shape[1], P['cm_conv_w'].shape[1], P['sc_conv_w'].shape[1])
    halos = tuple(-(-(wd - 1) // SUBLANES) * SUBLANES for wd in widths)
    tt = _pick_tile(t, 256, SUBLANES)
    n_t = t // tt
    nb = P['lru_gate_w'].shape[2]
    blk = w // nb
    zspec = lambda slot: pl.BlockSpec((None, tt, w), lambda bi, ti: (bi, ti, slot))
    bspec = lambda rows: pl.BlockSpec((None, rows, w), lambda bi, ti: (bi, 0, 0))
    lspec = lambda rows: pl.BlockSpec((None, rows, w), lambda bi, ti: (layer, 0, 0))
    in_specs = [zspec(s) for s in range(3, 10)]
    in_specs += [bspec(1), bspec(halos[0]), bspec(halos[1]), bspec(halos[2])]
    in_specs += [lspec(widths[0]), lspec(1),
                 pl.BlockSpec((None, 2, nb, blk, blk), lambda bi, ti: (layer, 0, 0, 0, 0)),
                 lspec(2), lspec(1), lspec(widths[1]), lspec(1), lspec(1), lspec(1), lspec(widths[2])]
    out_specs = [pl.BlockSpec((None, tt, 3 * w), lambda bi, ti: (bi, ti, 0)),
                 bspec(1), bspec(widths[0] - 1), bspec(widths[1] - 1), bspec(widths[2] - 1)]
    out_shape = [jax.ShapeDtypeStruct((b, t, 3 * w), BF16),
                 jax.ShapeDtypeStruct((b, 1, w), F32),
                 jax.ShapeDtypeStruct((b, widths[0] - 1, w), F32),
                 jax.ShapeDtypeStruct((b, widths[1] - 1, w), F32),
                 jax.ShapeDtypeStruct((b, widths[2] - 1, w), F32)]
    row3 = lambda a: a.reshape(a.shape[0], 1, a.shape[1])
    return pl.pallas_call(
        functools.partial(_mixers_kernel, tt=tt, n_t=n_t, halos=halos, widths=widths, blk=blk),
        grid=(b, n_t),
        in_specs=in_specs,
        out_specs=out_specs,
        out_shape=out_shape,
        scratch_shapes=[pltpu.VMEM((halos[0] + tt, w), F32), pltpu.VMEM((halos[1] + tt, w), F32),
                        pltpu.VMEM((halos[2] + tt, w), F32), pltpu.VMEM((1, w), F32),
                        pltpu.VMEM((SUBLANES, w), F32)],
        compiler_params=_cparams("parallel", "arbitrary"),
        name="mixers",
    )(*([z] * 7), h0, _pad_state(lru_st, halos[0]), _pad_state(cm_st, halos[1]), _pad_state(sc_st, halos[2]),
      P['lru_conv_w'], row3(P['lru_conv_b']), P['lru_gate_w'], P['lru_gate_b'], row3(P['lru_a_param']),
      P['cm_conv_w'], row3(P['cm_conv_b']), row3(P['cm_ln_w']), row3(P['cm_ln_b']), P['sc_conv_w'])


def _layer(x, pos_tables, states, P, layer, lam_init, dims, paged=None):
    b, t, d = x.shape
    w, n_heads, head_dim = dims
    nw = P['norm_w']
    x2 = x.reshape(b * t, d)
    h = _norm_matmul(x2, nw, P['ffn1_w_gu'], layer, 0, swiglu=True, out_dtype=BF16)
    x2 = _matmul_norm_res([h], P['ffn1_w_down'], x2, nw, layer, 1, 0.5)
    z = _norm_matmul(x2, nw, P['w_in'], layer, 2, swiglu=False, out_dtype=F32).reshape(b, t, N_IN_SLOTS * w)
    cos, sin = pos_tables
    if paged is None:
        q, k_rot, k16, v16 = _rope(z, cos, sin, w=w, head_dim=head_dim, q_dtype=BF16)
        ya = _attn_prompt(q, k16, v16, P['attn_lambda'], P['attn_subln_w'], layer, lam_init,
                          n_heads=n_heads, head_dim=head_dim)
    else:
        cache_k, cache_v, page_table = paged
        q, k_rot, _, _ = _rope(z, cos, sin, w=w, head_dim=head_dim, q_dtype=F32)
        ya = _attn_paged(q, k_rot, z[:, :, 2 * w:3 * w], cache_k, cache_v, page_table, P['attn_lambda'],
                         P['attn_subln_w'], layer, lam_init, n_heads=n_heads, head_dim=head_dim)
    v_rows = z[:, :, 2 * w:3 * w]
    ybcd, h_t, lru_buf, cm_buf, sc_buf = _mixers(z, states, P, layer, w=w)
    x2 = _matmul_norm_res([ya.reshape(b * t, w), ybcd.reshape(b * t, 3 * w)], P['w_out'], x2, nw, layer, 3, 1.0)
    h = _norm_matmul(x2, nw, P['ffn2_w_gu'], layer, 4, swiglu=True, out_dtype=BF16)
    x2 = _matmul_norm_res([h], P['ffn2_w_down'], x2, nw, layer, 5, 0.5)
    e = 2 * head_dim
    outs = (k_rot.reshape(b, t, n_heads, e), v_rows.reshape(b, t, n_heads, e), h_t.reshape(b, w),
            lru_buf, cm_buf, sc_buf)
    return x2.reshape(b, t, d), outs


def kernel(x_prompt, x_sample, cache_k, cache_v, state_lru_h, state_lru_conv, state_cm_conv, state_sc_conv, page_table, norm_w, ffn1_w_gu, ffn1_w_down, ffn2_w_gu, ffn2_w_down, w_in, w_out, attn_lambda, attn_subln_w, lru_conv_w, lru_conv_b, lru_gate_w, lru_gate_b, lru_a_param, cm_conv_w, cm_conv_b, cm_ln_w, cm_ln_b, sc_conv_w):
    b, t, d = x_prompt.shape
    db, ts, _ = x_sample.shape
    depth = norm_w.shape[0]
    w = d // N_GROUPS
    n_heads = cache_k.shape[3]
    e = cache_k.shape[4]
    head_dim = e // 2
    assert n_heads * e == w and w_in.shape[2] == N_IN_SLOTS * w
    n_pool, page = cache_k.shape[1], cache_k.shape[2]
    past = page_table.shape[1] * page
    dims = (w, n_heads, head_dim)

    P = {
        'norm_w': norm_w,
        'ffn1_w_gu': ffn1_w_gu.astype(BF16), 'ffn1_w_down': ffn1_w_down.astype(BF16),
        'ffn2_w_gu': ffn2_w_gu.astype(BF16), 'ffn2_w_down': ffn2_w_down.astype(BF16),
        'w_in': w_in.astype(BF16), 'w_out': w_out.astype(BF16),
        'attn_lambda': attn_lambda,
        'attn_subln_w': attn_subln_w.reshape(depth, 1, e),
        'lru_conv_w': lru_conv_w, 'lru_conv_b': lru_conv_b, 'lru_gate_w': lru_gate_w,
        'lru_gate_b': lru_gate_b, 'lru_a_param': lru_a_param,
        'cm_conv_w': cm_conv_w, 'cm_conv_b': cm_conv_b, 'cm_ln_w': cm_ln_w, 'cm_ln_b': cm_ln_b,
        'sc_conv_w': sc_conv_w,
    }
    cache_k4 = cache_k.reshape(depth, n_pool, page * n_heads, e)
    cache_v4 = cache_v.reshape(depth, n_pool, page * n_heads, e)
    tables_p = _rope_tables(jnp.arange(t), w, head_dim)
    tables_s = _rope_tables(past + jnp.arange(ts), w, head_dim)
    zeros_p = (jnp.zeros((b, 1, w), F32), jnp.zeros((b, lru_conv_w.shape[1] - 1, w), F32),
               jnp.zeros((b, cm_conv_w.shape[1] - 1, w), F32), jnp.zeros((b, sc_conv_w.shape[1] - 1, w), F32))

    xp, xs = x_prompt, x_sample
    outs_p, outs_s = [], []
    for l in range(depth):
        lam_init = 0.8 - 0.6 * math.exp(-0.3 * l)
        xp, sp = _layer(xp, tables_p, zeros_p, P, l, lam_init, dims)
        outs_p.append(sp)
        st_s = (state_lru_h[l].reshape(db, 1, w), state_lru_conv[l], state_cm_conv[l], state_sc_conv[l])
        xs, ss = _layer(xs, tables_s, st_s, P, l, lam_init, dims, paged=(cache_k4, cache_v4, page_table))
        outs_s.append(ss)
    stack = lambda outs, i: jnp.stack([o[i] for o in outs])
    return (xp, xs) + tuple(stack(outs_p, i) for i in range(6)) + tuple(stack(outs_s, i) for i in range(6))
```

```python
import functools
import math

import jax
import jax.numpy as jnp
from jax import lax
from jax.experimental import pallas as pl
from jax.experimental.pallas import tpu as pltpu

F32 = jnp.float32
BF16 = jnp.bfloat16

NORM_EPS = 1e-6
ROPE_THETA = 10000.0
LRU_C = 8.0
N_GROUPS = 4
N_IN_SLOTS = 10
VMEM_LIMIT_BYTES = 56 * 1024 * 1024
LANES = 128
SUBLANES = 8


def _cparams(*sem):
    return pltpu.CompilerParams(dimension_semantics=sem, vmem_limit_bytes=VMEM_LIMIT_BYTES)


def _pick_tile(n, target, quantum):
    if n <= target:
        return n
    t = (target // quantum) * quantum
    while t >= quantum:
        if n % t == 0:
            return t
        t -= quantum
    return n


def _dot(a, b):
    return jnp.dot(a, b, preferred_element_type=F32)


def _dot_nt(a, b):
    return lax.dot_general(a, b, (((1,), (1,)), ((), ())), preferred_element_type=F32)


def _rms_rows(x, w_row):
    ms = jnp.mean(x * x, axis=-1, keepdims=True)
    return x * lax.rsqrt(ms + NORM_EPS) * w_row


def _side_cast_plan(srcs, n_steps, step_index):
    in_specs, out_specs, out_shapes = [], [], []
    for src, layer in srcs:
        _, r, c = src.shape
        nb = next(n for n in range(min(n_steps, r // 16), 0, -1) if r % n == 0 and (r // n) % 16 == 0)
        rb = r // nb

        def in_map(*ids, layer=layer, nb=nb):
            return (layer, jnp.minimum(step_index(*ids), nb - 1), 0)

        def out_map(*ids, nb=nb):
            return (jnp.minimum(step_index(*ids), nb - 1), 0)

        in_specs.append(pl.BlockSpec((None, rb, c), in_map))
        out_specs.append(pl.BlockSpec((rb, c), out_map))
        out_shapes.append(jax.ShapeDtypeStruct((r, c), BF16))
    return in_specs, out_specs, out_shapes


def _run_side_casts(src_refs, dst_refs):
    for src_ref, dst_ref in zip(src_refs, dst_refs):
        dst_ref[...] = src_ref[...].astype(BF16)


def _norm_matmul_kernel(x_ref, nw_ref, *refs, norm_row, swiglu, n_side):
    n_w = 2 if swiglu else 1
    w_refs = refs[:n_w]
    side_src = refs[n_w:n_w + n_side]
    out_ref = refs[n_w + n_side]
    side_dst = refs[n_w + n_side + 1:n_w + 2 * n_side + 1]
    xn_ref = refs[-1]
    j = pl.program_id(1)

    @pl.when(j == 0)
    def _():
        xn_ref[...] = _rms_rows(x_ref[...], nw_ref[norm_row:norm_row + 1, :]).astype(BF16)

    xn = xn_ref[...]
    if swiglu:
        g = _dot(xn, w_refs[0][...])
        u = _dot(xn, w_refs[1][...])
        out_ref[...] = (g * jax.nn.sigmoid(g) * u).astype(out_ref.dtype)
    else:
        out_ref[...] = _dot(xn, w_refs[0][...]).astype(out_ref.dtype)
    _run_side_casts(side_src, side_dst)


def _norm_matmul(x, norm_w, w, layer, norm_row, *, swiglu, out_dtype, side=()):
    m, d = x.shape
    n = w.shape[1]
    n_out = n // 2 if swiglu else n
    tm = _pick_tile(m, 1024, 16)
    tn = _pick_tile(n_out, 512, LANES)
    n_blocks = n_out // tn
    grid = (m // tm, n_blocks)
    in_specs = [
        pl.BlockSpec((tm, d), lambda i, j: (i, 0)),
        pl.BlockSpec((None, norm_w.shape[1], d), lambda i, j: (layer, 0, 0)),
        pl.BlockSpec((d, tn), lambda i, j: (0, j)),
    ]
    args = [x, norm_w, w]
    if swiglu:
        in_specs.append(pl.BlockSpec((d, tn), lambda i, j: (0, j + n_blocks)))
        args.append(w)
    s_in, s_out, s_shapes = _side_cast_plan(side, grid[0] * grid[1], lambda i, j: i * n_blocks + j)
    outs = pl.pallas_call(
        functools.partial(_norm_matmul_kernel, norm_row=norm_row, swiglu=swiglu, n_side=len(side)),
        grid=grid,
        in_specs=in_specs + s_in,
        out_specs=[pl.BlockSpec((tm, tn), lambda i, j: (i, j))] + s_out,
        out_shape=[jax.ShapeDtypeStruct((m, n_out), out_dtype)] + s_shapes,
        scratch_shapes=[pltpu.VMEM((tm, d), BF16)],
        compiler_params=_cparams("arbitrary", "arbitrary"),
        name="norm_matmul_swiglu" if swiglu else "norm_matmul",
    )(*args, *[src for src, _ in side])
    return outs[0], outs[1:]


def _matmul_norm_res_kernel(*refs, n_parts, part_rows, norm_row, scale, n_side):
    h_refs = refs[:n_parts]
    w_ref, x_ref, nw_ref = refs[n_parts:n_parts + 3]
    side_src = refs[n_parts + 3:n_parts + 3 + n_side]
    out_ref = refs[n_parts + 3 + n_side]
    side_dst = refs[n_parts + 4 + n_side:]
    y = None
    row = 0
    for h_ref, rows in zip(h_refs, part_rows):
        part = _dot(h_ref[...], w_ref[row:row + rows, :])
        y = part if y is None else y + part
        row += rows
    yn = _rms_rows(y, nw_ref[norm_row:norm_row + 1, :])
    out_ref[...] = x_ref[...] + scale * yn
    _run_side_casts(side_src, side_dst)


def _matmul_norm_res(h_parts, w, x, norm_w, layer, norm_row, scale, side=()):
    m, d = x.shape
    k = w.shape[0]
    part_rows = tuple(h.shape[1] for h in h_parts)
    assert sum(part_rows) == k
    tm = _pick_tile(m, 256 if k > 4096 else 512, 16)
    in_specs = [pl.BlockSpec((tm, r), lambda i: (i, 0)) for r in part_rows]
    in_specs += [
        pl.BlockSpec((k, d), lambda i: (0, 0), pipeline_mode=pl.Buffered(1)),
        pl.BlockSpec((tm, d), lambda i: (i, 0)),
        pl.BlockSpec((None, norm_w.shape[1], d), lambda i: (layer, 0, 0)),
    ]
    s_in, s_out, s_shapes = _side_cast_plan(side, m // tm, lambda i: i)
    outs = pl.pallas_call(
        functools.partial(_matmul_norm_res_kernel, n_parts=len(h_parts), part_rows=part_rows,
                          norm_row=norm_row, scale=scale, n_side=len(side)),
        grid=(m // tm,),
        in_specs=in_specs + s_in,
        out_specs=[pl.BlockSpec((tm, d), lambda i: (i, 0))] + s_out,
        out_shape=[jax.ShapeDtypeStruct((m, d), F32)] + s_shapes,
        compiler_params=_cparams("arbitrary"),
        name="matmul_norm_res",
    )(*h_parts, w, x, norm_w, *[src for src, _ in side])
    return outs[0], outs[1:]


def _swap_halves(x, half):
    n = x.shape[-1]
    lane = lax.broadcasted_iota(jnp.int32, x.shape, x.ndim - 1)
    first = (lane & half) == 0
    return jnp.where(first, pltpu.roll(x, n - half, x.ndim - 1), pltpu.roll(x, half, x.ndim - 1))


def _rope_kernel(q_ref, k_ref, v_ref, cos_ref, sin_ref, qo_ref, ko_ref, kb_ref, vb_ref, *, half, q_scale):
    cos = cos_ref[...]
    sin = sin_ref[...]
    q = q_ref[...]
    k = k_ref[...]
    qr = q * cos + _swap_halves(q, half) * sin
    kr = k * cos + _swap_halves(k, half) * sin
    qo_ref[...] = (qr * q_scale).astype(qo_ref.dtype)
    ko_ref[...] = kr
    kb_ref[...] = kr.astype(BF16)
    vb_ref[...] = v_ref[...].astype(BF16)


def _rope(z, cos, sin, *, w, head_dim, q_dtype):
    b, t, _ = z.shape
    tt = _pick_tile(t, 512, 16)
    zspec = lambda slot: pl.BlockSpec((None, tt, w), lambda bi, ti: (bi, ti, slot))
    tspec = pl.BlockSpec((tt, w), lambda bi, ti: (ti, 0))
    ospec = pl.BlockSpec((None, tt, w), lambda bi, ti: (bi, ti, 0))
    return pl.pallas_call(
        functools.partial(_rope_kernel, half=head_dim // 2, q_scale=head_dim ** -0.5),
        grid=(b, t // tt),
        in_specs=[zspec(0), zspec(1), zspec(2), tspec, tspec],
        out_specs=[ospec, ospec, ospec, ospec],
        out_shape=[jax.ShapeDtypeStruct((b, t, w), q_dtype), jax.ShapeDtypeStruct((b, t, w), F32),
                   jax.ShapeDtypeStruct((b, t, w), BF16), jax.ShapeDtypeStruct((b, t, w), BF16)],
        compiler_params=_cparams("parallel", "parallel"),
        name="rope",
    )(z, z, z, cos, sin)


def _rope_tables(pos, w, head_dim):
    half = head_dim // 2
    inv = ROPE_THETA ** (-jnp.arange(half, dtype=F32) / half)
    ang = pos.astype(F32)[:, None] * inv[None, :]
    cos = jnp.cos(ang)
    sin = jnp.sin(ang)
    cos_d = jnp.concatenate([cos, cos], axis=-1)
    sin_d = jnp.concatenate([-sin, sin], axis=-1)
    reps = w // head_dim
    return jnp.tile(cos_d, (1, reps)), jnp.tile(sin_d, (1, reps))


def _lambda_value(lp, lam_init):
    s1 = jnp.sum(lp[0:1, :] * lp[1:2, :], axis=-1, keepdims=True)
    s2 = jnp.sum(lp[2:3, :] * lp[3:4, :], axis=-1, keepdims=True)
    return jnp.exp(s1) - jnp.exp(s2) + lam_init


def _attn_prompt_kernel(q_ref, k_ref, v_ref, lp_ref, sw_ref, o_ref, *, tile, head_dim, lam_init):
    t, e = q_ref.shape
    lam = _lambda_value(lp_ref[...], lam_init)
    lane = lax.broadcasted_iota(jnp.int32, (tile, e), 1)
    row = lax.broadcasted_iota(jnp.int32, (tile, tile), 0)
    col = lax.broadcasted_iota(jnp.int32, (tile, tile), 1)
    keep = col <= row
    for i in range(t // tile):
        kv = (i + 1) * tile
        q = q_ref[i * tile:kv, :]
        ks = k_ref[0:kv, :]
        vx = jnp.concatenate([v_ref[0:kv, :], jnp.ones((kv, e), BF16)], axis=1)
        normed = []
        for c in range(2):
            qc = jnp.where((lane >= c * head_dim) & (lane < (c + 1) * head_dim), q, jnp.zeros_like(q))
            s = _dot_nt(qc, ks)
            s_diag = jnp.where(keep, s[:, i * tile:], -jnp.inf)
            m = jnp.max(s_diag, axis=-1, keepdims=True)
            if i > 0:
                s_full = s[:, :i * tile]
                m = jnp.maximum(m, jnp.max(s_full, axis=-1, keepdims=True))
                p = jnp.concatenate([jnp.exp(s_full - m), jnp.exp(s_diag - m)], axis=1)
            else:
                p = jnp.exp(s_diag - m)
            acc = _dot(p.astype(BF16), vx)
            normed.append(acc[:, :e] / acc[:, e:])
        o = normed[0] - lam * normed[1]
        o_ref[i * tile:kv, :] = (_rms_rows(o, sw_ref[...]) * (1.0 - lam_init)).astype(o_ref.dtype)


def _attn_prompt(q, k, v, attn_lambda, subln_w, layer, lam_init, *, n_heads, head_dim):
    b, t, w = q.shape
    e = 2 * head_dim
    tile = _pick_tile(t, 256, LANES)
    head_spec = pl.BlockSpec((None, t, e), lambda bi, h: (bi, 0, h))
    return pl.pallas_call(
        functools.partial(_attn_prompt_kernel, tile=tile, head_dim=head_dim, lam_init=lam_init),
        grid=(b, n_heads),
        in_specs=[
            head_spec, head_spec, head_spec,
            pl.BlockSpec((None, 4, head_dim), lambda bi, h: (layer, 0, 0)),
            pl.BlockSpec((None, 1, e), lambda bi, h: (layer, 0, 0)),
        ],
        out_specs=head_spec,
        out_shape=jax.ShapeDtypeStruct((b, t, w), BF16),
        compiler_params=_cparams("parallel", "parallel"),
        name="attn_prompt",
    )(q, k, v, attn_lambda, subln_w)


def _attn_paged_kernel(pt_ref, q_ref, kn_ref, vn_ref, lp_ref, sw_ref, ck_ref, cv_ref, o_ref,
                       kbuf, vbuf, sem, qb_ref, m_ref, l_ref, acc_ref,
                       *, layer, pps, n_heads, head_dim, ts, lam_init):
    step = pl.program_id(1)
    n_steps = pl.num_programs(1)
    flat = pl.program_id(0) * n_steps + step
    total = pl.num_programs(0) * n_steps
    slot = flat % 2
    e = 2 * head_dim
    half = n_heads * ts
    rows = 2 * half
    page_rows = kbuf.shape[2]

    def page_copies(b_idx, s_idx, slot_idx):
        out = []
        for r in range(pps):
            page_id = pt_ref[b_idx, s_idx * pps + r]
            out.append(pltpu.make_async_copy(ck_ref.at[layer, page_id], kbuf.at[slot_idx, r], sem.at[slot_idx, 0]))
            out.append(pltpu.make_async_copy(cv_ref.at[layer, page_id], vbuf.at[slot_idx, r], sem.at[slot_idx, 1]))
        return out

    @pl.when(flat == 0)
    def _():
        for c in page_copies(0, 0, 0):
            c.start()

    nxt = jnp.where(flat + 1 < total, flat + 1, 0)
    for c in page_copies(nxt // n_steps, nxt % n_steps, 1 - slot):
        c.start()
    for c in page_copies(pl.program_id(0), step, slot):
        c.wait()

    row_i = lax.broadcasted_iota(jnp.int32, (rows, e), 0)
    lane_i = lax.broadcasted_iota(jnp.int32, (rows, e), 1)
    row_c = row_i // half
    row_h = (row_i % half) // ts
    row_t = row_i % ts

    @pl.when(step == 0)
    def _():
        own_comp = (lane_i // head_dim) == row_c
        qb = jnp.zeros((rows, e), F32)
        for h in range(n_heads):
            for t in range(ts):
                qb = jnp.where(own_comp & (row_h == h) & (row_t == t), q_ref[t:t + 1, h * e:(h + 1) * e], qb)
        qb_ref[...] = qb
        m_ref[...] = jnp.full(m_ref.shape, -jnp.inf, F32)
        l_ref[...] = jnp.zeros(l_ref.shape, F32)
        acc_ref[...] = jnp.zeros(acc_ref.shape, F32)

    qb = qb_ref[...]
    qb16 = qb.astype(BF16)
    col_h = lax.broadcasted_iota(jnp.int32, (rows, page_rows), 1) % n_heads
    own_cols = col_h == (lax.broadcasted_iota(jnp.int32, (rows, page_rows), 0) % half) // ts
    s = [jnp.where(own_cols, _dot_nt(qb16, kbuf[slot, r].astype(BF16)), -jnp.inf) for r in range(pps)]
    m_old = m_ref[...]
    m_new = m_old
    for r in range(pps):
        m_new = jnp.maximum(m_new, jnp.max(s[r], axis=-1, keepdims=True))
    alpha = jnp.exp(m_old - m_new)
    l_new = alpha * l_ref[...]
    acc = alpha * acc_ref[...]
    for r in range(pps):
        p = jnp.exp(s[r] - m_new)
        l_new = l_new + jnp.sum(p, axis=-1, keepdims=True)
        acc = acc + _dot(p.astype(BF16), vbuf[slot, r].astype(BF16))
    m_ref[...] = m_new
    l_ref[...] = l_new
    acc_ref[...] = acc

    @pl.when(flat == total - 1)
    def _():
        for c in page_copies(0, 0, 1 - slot):
            c.wait()

    @pl.when(step == n_steps - 1)
    def _():
        m = m_ref[...]
        l = l_ref[...]
        a = acc_ref[...]

        def own_head_rows(ref, t):
            out = jnp.zeros((rows, e), F32)
            for h in range(n_heads):
                out = jnp.where(row_h == h, ref[t:t + 1, h * e:(h + 1) * e], out)
            return out

        row_t1 = row_t[:, 0:1]
        s_new = []
        for t in range(ts):
            s_t = jnp.sum(qb * own_head_rows(kn_ref, t), axis=-1, keepdims=True)
            s_new.append(jnp.where(row_t1 >= t, s_t, -jnp.inf))
        m_fin = m
        for t in range(ts):
            m_fin = jnp.maximum(m_fin, s_new[t])
        alpha_f = jnp.exp(m - m_fin)
        l = alpha_f * l
        a = alpha_f * a
        for t in range(ts):
            p_t = jnp.exp(s_new[t] - m_fin)
            l = l + p_t
            a = a + p_t * own_head_rows(vn_ref, t)
        d = a / l
        lam = _lambda_value(lp_ref[...], lam_init)
        o = d[0:half, :] - lam * d[half:rows, :]
        on = _rms_rows(o, sw_ref[...]) * (1.0 - lam_init)
        for h in range(n_heads):
            o_ref[:, h * e:(h + 1) * e] = on[h * ts:(h + 1) * ts, :].astype(o_ref.dtype)


def _attn_paged(q, k_new, v_new, cache_k, cache_v, page_table, attn_lambda, subln_w, layer, lam_init,
                *, n_heads, head_dim):
    db, ts, w = q.shape
    e = 2 * head_dim
    page_rows = cache_k.shape[2]
    n_pages = page_table.shape[1]
    pps = _pick_tile(n_pages, 16, 1)
    rows = 2 * n_heads * ts
    tok_spec = pl.BlockSpec((None, ts, w), lambda b, s, pt: (b, 0, 0))
    grid_spec = pltpu.PrefetchScalarGridSpec(
        num_scalar_prefetch=1,
        grid=(db, n_pages // pps),
        in_specs=[tok_spec, tok_spec, tok_spec,
                  pl.BlockSpec((None, 4, head_dim), lambda b, s, pt: (layer, 0, 0)),
                  pl.BlockSpec((None, 1, e), lambda b, s, pt: (layer, 0, 0)),
                  pl.BlockSpec(memory_space=pl.ANY), pl.BlockSpec(memory_space=pl.ANY)],
        out_specs=pl.BlockSpec((None, ts, w), lambda b, s, pt: (b, 0, 0)),
        scratch_shapes=[pltpu.VMEM((2, pps, page_rows, e), F32), pltpu.VMEM((2, pps, page_rows, e), F32),
                        pltpu.SemaphoreType.DMA((2, 2)),
                        pltpu.VMEM((rows, e), F32), pltpu.VMEM((rows, 1), F32),
                        pltpu.VMEM((rows, 1), F32), pltpu.VMEM((rows, e), F32)],
    )
    return pl.pallas_call(
        functools.partial(_attn_paged_kernel, layer=layer, pps=pps, n_heads=n_heads, head_dim=head_dim, ts=ts,
                          lam_init=lam_init),
        grid_spec=grid_spec,
        out_shape=jax.ShapeDtypeStruct((db, ts, w), BF16),
        compiler_params=_cparams("arbitrary", "arbitrary"),
        name="attn_paged",
    )(page_table, q, k_new, v_new, attn_lambda, subln_w, cache_k, cache_v)


def _gelu_tanh(x):
    return 0.5 * x * (1.0 + jnp.tanh(math.sqrt(2.0 / math.pi) * (x + 0.044715 * (x * x * x))))


def _conv_from_buffer(buf_ref, w_ref, *, halo, width, rows, row_block):
    first = halo - (width - 1)
    outs = []
    for r0 in range(0, rows, row_block):
        rb = min(row_block, rows - r0)
        acc = None
        for phase in range(min(SUBLANES, width)):
            taps = list(range(phase, width, SUBLANES))
            span = rb + (taps[-1] - phase)
            zs = buf_ref[pl.ds(first + r0 + phase, span), :]
            part = None
            for j in taps:
                term = w_ref[j:j + 1, :] * zs[j - phase:j - phase + rb, :]
                part = term if part is None else part + term
            acc = part if acc is None else acc + part
        outs.append(acc)
    return outs[0] if len(outs) == 1 else jnp.concatenate(outs, axis=0)


def _mixers_kernel(xb_ref, gb_ref, ca_ref, cb_ref, sb_ref, sc_ref, sx_ref,
                   h0_ref, lru_st_ref, cm_st_ref, sc_st_ref,
                   lcw_ref, lcb_ref, gw_ref, gbias_ref, ap_ref,
                   ccw_ref, ccb_ref, lnw_ref, lnb_ref, scw_ref,
                   y_ref, h_out_ref, lru_out_ref, cm_out_ref, sc_out_ref,
                   lru_buf, cm_buf, sc_buf, h_ref, hs_buf,
                   *, tt, n_t, halos, widths, blk):
    ti = pl.program_id(1)
    w = xb_ref.shape[-1]
    (halo_l, halo_c, halo_s), (wid_l, wid_c, wid_s) = halos, widths

    @pl.when(ti == 0)
    def _():
        lru_buf[0:halo_l, :] = lru_st_ref[...]
        cm_buf[0:halo_c, :] = cm_st_ref[...]
        sc_buf[0:halo_s, :] = sc_st_ref[...]
        h_ref[...] = h0_ref[...]

    lru_buf[halo_l:halo_l + tt, :] = xb_ref[...]
    xc = _conv_from_buffer(lru_buf, lcw_ref, halo=halo_l, width=wid_l, rows=tt, row_block=tt) + lcb_ref[...]
    xc16 = xc.astype(BF16)
    gates = []
    for g in range(2):
        parts = [_dot(xc16[:, n * blk:(n + 1) * blk], gw_ref[g, n].astype(BF16)) for n in range(w // blk)]
        gates.append(jnp.concatenate(parts, axis=1) + gbias_ref[g:g + 1, :])
    r_gate = jax.nn.sigmoid(gates[0])
    i_gate = jax.nn.sigmoid(gates[1])
    log_a = -LRU_C * r_gate * jax.nn.softplus(-ap_ref[...])
    a = jnp.exp(log_a)
    u = jnp.sqrt(1.0 - jnp.exp(2.0 * log_a)) * (i_gate * xc)
    if tt % SUBLANES == 0:
        groups = tt // SUBLANES
        a3 = a.reshape(groups, SUBLANES, w)
        u3 = u.reshape(groups, SUBLANES, w)
        sub = lax.broadcasted_iota(jnp.int32, (groups, SUBLANES, w), 1)
        dist = 1
        while dist < SUBLANES:
            live = sub >= dist
            a_prev = jnp.where(live, pltpu.roll(a3, dist, 1), 1.0)
            u_prev = jnp.where(live, pltpu.roll(u3, dist, 1), 0.0)
            u3 = a3 * u_prev + u3
            a3 = a3 * a_prev
            dist *= 2
        h_last = h_ref[...]
        blocks = []
        for g in range(groups):
            blk_h = u3[g] + a3[g] * h_last
            blocks.append(blk_h)
            h_last = blk_h[SUBLANES - 1:SUBLANES, :]
        hs = jnp.concatenate(blocks, axis=0)
    else:
        h_last = h_ref[...]
        for r in range(tt):
            h_last = a[r:r + 1, :] * h_last + u[r:r + 1, :]
            hs_buf[r:r + 1, :] = h_last
        hs = hs_buf[0:tt, :]
    h_ref[...] = h_last
    y_ref[:, 0:w] = (_gelu_tanh(gb_ref[...]) * hs).astype(y_ref.dtype)

    cm_buf[halo_c:halo_c + tt, :] = ca_ref[...] * jax.nn.sigmoid(cb_ref[...])
    uc = _conv_from_buffer(cm_buf, ccw_ref, halo=halo_c, width=wid_c, rows=tt, row_block=32) + ccb_ref[...]
    mu = jnp.mean(uc, axis=-1, keepdims=True)
    cen = uc - mu
    var = jnp.mean(cen * cen, axis=-1, keepdims=True)
    ln = cen * lax.rsqrt(var + NORM_EPS) * lnw_ref[...] + lnb_ref[...]
    y_ref[:, w:2 * w] = (ln * jax.nn.sigmoid(ln)).astype(y_ref.dtype)

    sc_buf[halo_s:halo_s + tt, :] = sc_ref[...] * sx_ref[...]
    sconv = _conv_from_buffer(sc_buf, scw_ref, halo=halo_s, width=wid_s, rows=tt, row_block=tt)
    y_ref[:, 2 * w:3 * w] = (sb_ref[...] * sconv).astype(y_ref.dtype)

    @pl.when(ti == n_t - 1)
    def _():
        h_out_ref[...] = h_last
        lru_out_ref[...] = lru_buf[pl.ds(tt + halo_l - (wid_l - 1), wid_l - 1), :]
        cm_out_ref[...] = cm_buf[pl.ds(tt + halo_c - (wid_c - 1), wid_c - 1), :]
        sc_out_ref[...] = sc_buf[pl.ds(tt + halo_s - (wid_s - 1), wid_s - 1), :]

    if n_t > 1:
        @pl.when(ti < n_t - 1)
        def _():
            lru_buf[0:halo_l, :] = lru_buf[tt:tt + halo_l, :]
            cm_buf[0:halo_c, :] = cm_buf[tt:tt + halo_c, :]
            sc_buf[0:halo_s, :] = sc_buf[tt:tt + halo_s, :]


def _pad_state(st, halo):
    return jnp.pad(st, ((0, 0), (halo - st.shape[1], 0), (0, 0)))


def _mixers(z, states, P, layer, *, w):
    b, t, _ = z.shape
    h0, lru_st, cm_st, sc_st = states
    widths = (P['lru_conv_w'].shape[1], P['cm_conv_w'].shape[1], P['sc_conv_w'].shape[1])
    halos = tuple(-(-(wd - 1) // SUBLANES) * SUBLANES for wd in widths)
    tt = _pick_tile(t, 256, SUBLANES)
    n_t = t // tt
    nb = P['lru_gate_w'].shape[2]
    blk = w // nb
    zspec = lambda slot: pl.BlockSpec((None, tt, w), lambda bi, ti: (bi, ti, slot))
    bspec = lambda rows: pl.BlockSpec((None, rows, w), lambda bi, ti: (bi, 0, 0))
    lspec = lambda rows: pl.BlockSpec((None, rows, w), lambda bi, ti: (layer, 0, 0))
    in_specs = [zspec(s) for s in range(3, 10)]
    in_specs += [bspec(1), bspec(halos[0]), bspec(halos[1]), bspec(halos[2])]
    in_specs += [lspec(widths[0]), lspec(1),
                 pl.BlockSpec((None, 2, nb, blk, blk), lambda bi, ti: (layer, 0, 0, 0, 0)),
                 lspec(2), lspec(1), lspec(widths[1]), lspec(1), lspec(1), lspec(1), lspec(widths[2])]
    out_specs = [pl.BlockSpec((None, tt, 3 * w), lambda bi, ti: (bi, ti, 0)),
                 bspec(1), bspec(widths[0] - 1), bspec(widths[1] - 1), bspec(widths[2] - 1)]
    out_shape = [jax.ShapeDtypeStruct((b, t, 3 * w), BF16),
                 jax.ShapeDtypeStruct((b, 1, w), F32),
                 jax.ShapeDtypeStruct((b, widths[0] - 1, w), F32),
                 jax.ShapeDtypeStruct((b, widths[1] - 1, w), F32),
                 jax.ShapeDtypeStruct((b, widths[2] - 1, w), F32)]
    row3 = lambda a: a.reshape(a.shape[0], 1, a.shape[1])
    return pl.pallas_call(
        functools.partial(_mixers_kernel, tt=tt, n_t=n_t, halos=halos, widths=widths, blk=blk),
        grid=(b, n_t),
        in_specs=in_specs,
        out_specs=out_specs,
        out_shape=out_shape,
        scratch_shapes=[pltpu.VMEM((halos[0] + tt, w), F32), pltpu.VMEM((halos[1] + tt, w), F32),
                        pltpu.VMEM((halos[2] + tt, w), F32), pltpu.VMEM((1, w), F32),
                        pltpu.VMEM((SUBLANES, w), F32)],
        compiler_params=_cparams("parallel", "arbitrary"),
        name="mixers",
    )(*([z] * 7), h0, _pad_state(lru_st, halos[0]), _pad_state(cm_st, halos[1]), _pad_state(sc_st, halos[2]),
      P['lru_conv_w'], row3(P['lru_conv_b']), P['lru_gate_w'], P['lru_gate_b'], row3(P['lru_a_param']),
      P['cm_conv_w'], row3(P['cm_conv_b']), row3(P['cm_ln_w']), row3(P['cm_ln_b']), P['sc_conv_w'])


def _layer(x, pos_tables, states, P, wb, layer, lam_init, dims, paged=None, cast_next=None):
    b, t, d = x.shape
    w, n_heads, head_dim = dims
    nw = P['norm_w']
    casting = 'wd1' not in wb
    wb = dict(wb)
    x2 = x.reshape(b * t, d)
    side = lambda *names: [(P[n], layer) for n in names] if casting else []
    h, cast = _norm_matmul(x2, nw, wb['gu1'], layer, 0, swiglu=True, out_dtype=BF16,
                           side=side('ffn1_w_down', 'w_in'))
    if casting:
        wb['wd1'], wb['w_in'] = cast
    x2, cast = _matmul_norm_res([h], wb['wd1'], x2, nw, layer, 1, 0.5, side=side('ffn2_w_gu'))
    if casting:
        wb['gu2'], = cast
    z, cast = _norm_matmul(x2, nw, wb['w_in'], layer, 2, swiglu=False, out_dtype=F32, side=side('w_out'))
    if casting:
        wb['w_out'], = cast
    z = z.reshape(b, t, N_IN_SLOTS * w)
    cos, sin = pos_tables
    if paged is None:
        q, k_rot, k16, v16 = _rope(z, cos, sin, w=w, head_dim=head_dim, q_dtype=BF16)
        ya = _attn_prompt(q, k16, v16, P['attn_lambda'], P['attn_subln_w'], layer, lam_init,
                          n_heads=n_heads, head_dim=head_dim)
    else:
        cache_k, cache_v, page_table = paged
        q, k_rot, _, _ = _rope(z, cos, sin, w=w, head_dim=head_dim, q_dtype=F32)
        ya = _attn_paged(q, k_rot, z[:, :, 2 * w:3 * w], cache_k, cache_v, page_table, P['attn_lambda'],
                         P['attn_subln_w'], layer, lam_init, n_heads=n_heads, head_dim=head_dim)
    v_rows = z[:, :, 2 * w:3 * w]
    ybcd, h_t, lru_buf, cm_buf, sc_buf = _mixers(z, states, P, layer, w=w)
    x2, _ = _matmul_norm_res([ya.reshape(b * t, w), ybcd.reshape(b * t, 3 * w)], wb['w_out'], x2, nw, layer, 3, 1.0)
    h, cast = _norm_matmul(x2, nw, wb['gu2'], layer, 4, swiglu=True, out_dtype=BF16, side=side('ffn2_w_down'))
    if casting:
        wb['wd2'], = cast
    next_side = [(P['ffn1_w_gu'], cast_next)] if casting and cast_next is not None else []
    x2, cast = _matmul_norm_res([h], wb['wd2'], x2, nw, layer, 5, 0.5, side=next_side)
    if next_side:
        wb['next_gu1'], = cast
    e = 2 * head_dim
    outs = (k_rot.reshape(b, t, n_heads, e), v_rows.reshape(b, t, n_heads, e), h_t.reshape(b, w),
            lru_buf, cm_buf, sc_buf)
    return x2.reshape(b, t, d), outs, wb


def kernel(x_prompt, x_sample, cache_k, cache_v, state_lru_h, state_lru_conv, state_cm_conv, state_sc_conv, page_table, norm_w, ffn1_w_gu, ffn1_w_down, ffn2_w_gu, ffn2_w_down, w_in, w_out, attn_lambda, attn_subln_w, lru_conv_w, lru_conv_b, lru_gate_w, lru_gate_b, lru_a_param, cm_conv_w, cm_conv_b, cm_ln_w, cm_ln_b, sc_conv_w):
    b, t, d = x_prompt.shape
    db, ts, _ = x_sample.shape
    depth = norm_w.shape[0]
    w = d // N_GROUPS
    n_heads = cache_k.shape[3]
    e = cache_k.shape[4]
    head_dim = e // 2
    assert n_heads * e == w and w_in.shape[2] == N_IN_SLOTS * w
    n_pool, page = cache_k.shape[1], cache_k.shape[2]
    past = page_table.shape[1] * page
    dims = (w, n_heads, head_dim)

    P = {
        'norm_w': norm_w,
        'ffn1_w_gu': ffn1_w_gu, 'ffn1_w_down': ffn1_w_down, 'ffn2_w_gu': ffn2_w_gu,
        'ffn2_w_down': ffn2_w_down, 'w_in': w_in, 'w_out': w_out,
        'attn_lambda': attn_lambda,
        'attn_subln_w': attn_subln_w.reshape(depth, 1, e),
        'lru_conv_w': lru_conv_w, 'lru_conv_b': lru_conv_b, 'lru_gate_w': lru_gate_w,
        'lru_gate_b': lru_gate_b, 'lru_a_param': lru_a_param,
        'cm_conv_w': cm_conv_w, 'cm_conv_b': cm_conv_b, 'cm_ln_w': cm_ln_w, 'cm_ln_b': cm_ln_b,
        'sc_conv_w': sc_conv_w,
    }
    cache_k4 = cache_k.reshape(depth, n_pool, page * n_heads, e)
    cache_v4 = cache_v.reshape(depth, n_pool, page * n_heads, e)
    tables_p = _rope_tables(jnp.arange(t), w, head_dim)
    tables_s = _rope_tables(past + jnp.arange(ts), w, head_dim)
    zeros_p = (jnp.zeros((b, 1, w), F32), jnp.zeros((b, lru_conv_w.shape[1] - 1, w), F32),
               jnp.zeros((b, cm_conv_w.shape[1] - 1, w), F32), jnp.zeros((b, sc_conv_w.shape[1] - 1, w), F32))

    xp, xs = x_prompt, x_sample
    outs_p, outs_s = [], []
    gu1 = ffn1_w_gu[0].astype(BF16)
    for l in range(depth):
        lam_init = 0.8 - 0.6 * math.exp(-0.3 * l)
        xp, sp, wb = _layer(xp, tables_p, zeros_p, P, {'gu1': gu1}, l, lam_init, dims,
                            cast_next=l + 1 if l + 1 < depth else None)
        outs_p.append(sp)
        st_s = (state_lru_h[l].reshape(db, 1, w), state_lru_conv[l], state_cm_conv[l], state_sc_conv[l])
        xs, ss, _ = _layer(xs, tables_s, st_s, P, wb, l, lam_init, dims, paged=(cache_k4, cache_v4, page_table))
        outs_s.append(ss)
        gu1 = wb.get('next_gu1')
    stack = lambda outs, i: jnp.stack([o[i] for o in outs])
    return (xp, xs) + tuple(stack(outs_p, i) for i in range(6)) + tuple(stack(outs_s, i) for i in range(6))
```

```python
import functools
import math

import jax
import jax.numpy as jnp
from jax import lax
from jax.experimental import pallas as pl
from jax.experimental.pallas import tpu as pltpu

F32 = jnp.float32
BF16 = jnp.bfloat16

NORM_EPS = 1e-6
ROPE_THETA = 10000.0
LRU_C = 8.0
N_GROUPS = 4
N_IN_SLOTS = 10
VMEM_LIMIT_BYTES = 56 * 1024 * 1024
LANES = 128
SUBLANES = 8


def _cparams(*sem):
    return pltpu.CompilerParams(dimension_semantics=sem, vmem_limit_bytes=VMEM_LIMIT_BYTES)


def _pick_tile(n, target, quantum):
    if n <= target:
        return n
    t = (target // quantum) * quantum
    while t >= quantum:
        if n % t == 0:
            return t
        t -= quantum
    return n


def _dot(a, b):
    return jnp.dot(a, b, preferred_element_type=F32)


def _dot_nt(a, b):
    return lax.dot_general(a, b, (((1,), (1,)), ((), ())), preferred_element_type=F32)


def _rms_rows(x, w_row):
    ms = jnp.mean(x * x, axis=-1, keepdims=True)
    return x * lax.rsqrt(ms + NORM_EPS) * w_row


def _side_cast_plan(srcs, n_steps, step_index):
    in_specs, out_specs, out_shapes = [], [], []
    for src, layer in srcs:
        _, r, c = src.shape
        nb = next(n for n in range(min(n_steps, r // 16), 0, -1) if r % n == 0 and (r // n) % 16 == 0)
        rb = r // nb

        def in_map(*ids, layer=layer, nb=nb):
            return (layer, jnp.minimum(step_index(*ids), nb - 1), 0)

        def out_map(*ids, nb=nb):
            return (jnp.minimum(step_index(*ids), nb - 1), 0)

        in_specs.append(pl.BlockSpec((None, rb, c), in_map))
        out_specs.append(pl.BlockSpec((rb, c), out_map))
        out_shapes.append(jax.ShapeDtypeStruct((r, c), BF16))
    return in_specs, out_specs, out_shapes


def _run_side_casts(src_refs, dst_refs):
    for src_ref, dst_ref in zip(src_refs, dst_refs):
        dst_ref[...] = src_ref[...].astype(BF16)


def _norm_matmul_kernel(x_ref, nw_ref, *refs, norm_row, swiglu, n_side):
    n_w = 2 if swiglu else 1
    w_refs = refs[:n_w]
    side_src = refs[n_w:n_w + n_side]
    out_ref = refs[n_w + n_side]
    side_dst = refs[n_w + n_side + 1:n_w + 2 * n_side + 1]
    xn_ref = refs[-1]
    j = pl.program_id(1)

    @pl.when(j == 0)
    def _():
        xn_ref[...] = _rms_rows(x_ref[...], nw_ref[norm_row:norm_row + 1, :]).astype(BF16)

    xn = xn_ref[...]
    if swiglu:
        g = _dot(xn, w_refs[0][...])
        u = _dot(xn, w_refs[1][...])
        out_ref[...] = (g * jax.nn.sigmoid(g) * u).astype(out_ref.dtype)
    else:
        out_ref[...] = _dot(xn, w_refs[0][...]).astype(out_ref.dtype)
    _run_side_casts(side_src, side_dst)


def _norm_matmul(x, norm_w, w, layer, norm_row, *, swiglu, out_dtype, tn_target=512, side=()):
    m, d = x.shape
    n = w.shape[1]
    n_out = n // 2 if swiglu else n
    tm = _pick_tile(m, 1024, 16)
    if m <= 256:
        tn_target = 1536
    tn = _pick_tile(n_out, tn_target, LANES)
    n_blocks = n_out // tn
    grid = (m // tm, n_blocks)
    in_specs = [
        pl.BlockSpec((tm, d), lambda i, j: (i, 0)),
        pl.BlockSpec((None, norm_w.shape[1], d), lambda i, j: (layer, 0, 0)),
        pl.BlockSpec((d, tn), lambda i, j: (0, j)),
    ]
    args = [x, norm_w, w]
    if swiglu:
        in_specs.append(pl.BlockSpec((d, tn), lambda i, j: (0, j + n_blocks)))
        args.append(w)
    s_in, s_out, s_shapes = _side_cast_plan(side, grid[0] * grid[1], lambda i, j: i * n_blocks + j)
    outs = pl.pallas_call(
        functools.partial(_norm_matmul_kernel, norm_row=norm_row, swiglu=swiglu, n_side=len(side)),
        grid=grid,
        in_specs=in_specs + s_in,
        out_specs=[pl.BlockSpec((tm, tn), lambda i, j: (i, j))] + s_out,
        out_shape=[jax.ShapeDtypeStruct((m, n_out), out_dtype)] + s_shapes,
        scratch_shapes=[pltpu.VMEM((tm, d), BF16)],
        compiler_params=_cparams("arbitrary", "arbitrary"),
        name="norm_matmul_swiglu" if swiglu else "norm_matmul",
    )(*args, *[src for src, _ in side])
    return outs[0], outs[1:]


def _matmul_norm_res_kernel(*refs, n_parts, part_rows, norm_row, scale, n_side):
    h_refs = refs[:n_parts]
    w_ref, x_ref, nw_ref = refs[n_parts:n_parts + 3]
    side_src = refs[n_parts + 3:n_parts + 3 + n_side]
    out_ref = refs[n_parts + 3 + n_side]
    side_dst = refs[n_parts + 4 + n_side:]
    y = None
    row = 0
    for h_ref, rows in zip(h_refs, part_rows):
        part = _dot(h_ref[...], w_ref[row:row + rows, :])
        y = part if y is None else y + part
        row += rows
    yn = _rms_rows(y, nw_ref[norm_row:norm_row + 1, :])
    out_ref[...] = x_ref[...] + scale * yn
    _run_side_casts(side_src, side_dst)


def _matmul_norm_res(h_parts, w, x, norm_w, layer, norm_row, scale, side=()):
    m, d = x.shape
    k = w.shape[0]
    part_rows = tuple(h.shape[1] for h in h_parts)
    assert sum(part_rows) == k
    tm = _pick_tile(m, 256 if k > 4096 else 512, 16)
    in_specs = [pl.BlockSpec((tm, r), lambda i: (i, 0)) for r in part_rows]
    in_specs += [
        pl.BlockSpec((k, d), lambda i: (0, 0), pipeline_mode=pl.Buffered(1)),
        pl.BlockSpec((tm, d), lambda i: (i, 0)),
        pl.BlockSpec((None, norm_w.shape[1], d), lambda i: (layer, 0, 0)),
    ]
    s_in, s_out, s_shapes = _side_cast_plan(side, m // tm, lambda i: i)
    outs = pl.pallas_call(
        functools.partial(_matmul_norm_res_kernel, n_parts=len(h_parts), part_rows=part_rows,
                          norm_row=norm_row, scale=scale, n_side=len(side)),
        grid=(m // tm,),
        in_specs=in_specs + s_in,
        out_specs=[pl.BlockSpec((tm, d), lambda i: (i, 0))] + s_out,
        out_shape=[jax.ShapeDtypeStruct((m, d), F32)] + s_shapes,
        compiler_params=_cparams("arbitrary"),
        name="matmul_norm_res",
    )(*h_parts, w, x, norm_w, *[src for src, _ in side])
    return outs[0], outs[1:]


def _swap_halves(x, half):
    n = x.shape[-1]
    lane = lax.broadcasted_iota(jnp.int32, x.shape, x.ndim - 1)
    first = (lane & half) == 0
    return jnp.where(first, pltpu.roll(x, n - half, x.ndim - 1), pltpu.roll(x, half, x.ndim - 1))


def _rope_kernel(q_ref, k_ref, v_ref, cos_ref, sin_ref, qo_ref, ko_ref, vo_ref, kb_ref, vb_ref,
                 *, half, q_scale, n_heads):
    cos = cos_ref[...]
    sin = sin_ref[...]
    q = q_ref[...]
    k = k_ref[...]
    v = v_ref[...]
    qr = q * cos + _swap_halves(q, half) * sin
    kr = k * cos + _swap_halves(k, half) * sin
    qo_ref[...] = (qr * q_scale).astype(qo_ref.dtype)
    kb_ref[...] = kr.astype(BF16)
    vb_ref[...] = v.astype(BF16)
    tt = k.shape[0]
    e = k.shape[1] // n_heads
    for h in range(n_heads):
        ko_ref[pl.ds(h, tt, stride=n_heads), :] = kr[:, h * e:(h + 1) * e]
        vo_ref[pl.ds(h, tt, stride=n_heads), :] = v[:, h * e:(h + 1) * e]


def _rope(z, cos, sin, *, w, n_heads, head_dim, q_dtype):
    b, t, _ = z.shape
    e = w // n_heads
    tt = _pick_tile(t, 512, 16)
    zspec = lambda slot: pl.BlockSpec((None, tt, w), lambda bi, ti: (bi, ti, slot))
    tspec = pl.BlockSpec((tt, w), lambda bi, ti: (ti, 0))
    ospec = pl.BlockSpec((None, tt, w), lambda bi, ti: (bi, ti, 0))
    rspec = pl.BlockSpec((None, tt * n_heads, e), lambda bi, ti: (bi, ti, 0))
    rows_shape = jax.ShapeDtypeStruct((b, t * n_heads, e), F32)
    return pl.pallas_call(
        functools.partial(_rope_kernel, half=head_dim // 2, q_scale=head_dim ** -0.5, n_heads=n_heads),
        grid=(b, t // tt),
        in_specs=[zspec(0), zspec(1), zspec(2), tspec, tspec],
        out_specs=[ospec, rspec, rspec, ospec, ospec],
        out_shape=[jax.ShapeDtypeStruct((b, t, w), q_dtype), rows_shape, rows_shape,
                   jax.ShapeDtypeStruct((b, t, w), BF16), jax.ShapeDtypeStruct((b, t, w), BF16)],
        compiler_params=_cparams("parallel", "parallel"),
        name="rope",
    )(z, z, z, cos, sin)


def _rope_tables(pos, w, head_dim):
    half = head_dim // 2
    inv = ROPE_THETA ** (-jnp.arange(half, dtype=F32) / half)
    ang = pos.astype(F32)[:, None] * inv[None, :]
    cos = jnp.cos(ang)
    sin = jnp.sin(ang)
    cos_d = jnp.concatenate([cos, cos], axis=-1)
    sin_d = jnp.concatenate([-sin, sin], axis=-1)
    reps = w // head_dim
    return jnp.tile(cos_d, (1, reps)), jnp.tile(sin_d, (1, reps))


def _lambda_value(lp, lam_init):
    s1 = jnp.sum(lp[0:1, :] * lp[1:2, :], axis=-1, keepdims=True)
    s2 = jnp.sum(lp[2:3, :] * lp[3:4, :], axis=-1, keepdims=True)
    return jnp.exp(s1) - jnp.exp(s2) + lam_init


def _attn_prompt_kernel(q_ref, k_ref, v_ref, lp_ref, sw_ref, o_ref, *, tile, head_dim, lam_init):
    t, e = q_ref.shape
    lam = _lambda_value(lp_ref[...], lam_init)
    lane = lax.broadcasted_iota(jnp.int32, (tile, e), 1)
    row = lax.broadcasted_iota(jnp.int32, (tile, tile), 0)
    col = lax.broadcasted_iota(jnp.int32, (tile, tile), 1)
    keep = col <= row
    for i in range(t // tile):
        kv = (i + 1) * tile
        q = q_ref[i * tile:kv, :]
        ks = k_ref[0:kv, :]
        vx = jnp.concatenate([v_ref[0:kv, :], jnp.ones((kv, e), BF16)], axis=1)
        normed = []
        for c in range(2):
            qc = jnp.where((lane >= c * head_dim) & (lane < (c + 1) * head_dim), q, jnp.zeros_like(q))
            s = _dot_nt(qc, ks)
            s_diag = jnp.where(keep, s[:, i * tile:], -jnp.inf)
            m = jnp.max(s_diag, axis=-1, keepdims=True)
            if i > 0:
                s_full = s[:, :i * tile]
                m = jnp.maximum(m, jnp.max(s_full, axis=-1, keepdims=True))
                p = jnp.concatenate([jnp.exp(s_full - m), jnp.exp(s_diag - m)], axis=1)
            else:
                p = jnp.exp(s_diag - m)
            acc = _dot(p.astype(BF16), vx)
            normed.append(acc[:, :e] / acc[:, e:])
        o = normed[0] - lam * normed[1]
        o_ref[i * tile:kv, :] = (_rms_rows(o, sw_ref[...]) * (1.0 - lam_init)).astype(o_ref.dtype)


def _attn_prompt(q, k, v, attn_lambda, subln_w, layer, lam_init, *, n_heads, head_dim):
    b, t, w = q.shape
    e = 2 * head_dim
    tile = _pick_tile(t, 256, LANES)
    head_spec = pl.BlockSpec((None, t, e), lambda bi, h: (bi, 0, h))
    return pl.pallas_call(
        functools.partial(_attn_prompt_kernel, tile=tile, head_dim=head_dim, lam_init=lam_init),
        grid=(b, n_heads),
        in_specs=[
            head_spec, head_spec, head_spec,
            pl.BlockSpec((None, 4, head_dim), lambda bi, h: (layer, 0, 0)),
            pl.BlockSpec((None, 1, e), lambda bi, h: (layer, 0, 0)),
        ],
        out_specs=head_spec,
        out_shape=jax.ShapeDtypeStruct((b, t, w), BF16),
        compiler_params=_cparams("parallel", "parallel"),
        name="attn_prompt",
    )(q, k, v, attn_lambda, subln_w)


def _attn_paged_kernel(pt_ref, q_ref, kn_ref, vn_ref, lp_ref, sw_ref, ck_ref, cv_ref, o_ref,
                       kbuf, vbuf, sem, qb_ref, m_ref, l_ref, acc_ref,
                       *, layer, pps, n_heads, head_dim, ts, lam_init):
    step = pl.program_id(1)
    n_steps = pl.num_programs(1)
    flat = pl.program_id(0) * n_steps + step
    total = pl.num_programs(0) * n_steps
    slot = flat % 2
    e = 2 * head_dim
    half = n_heads * ts
    rows = 2 * half
    page_rows = kbuf.shape[2]

    def page_copies(b_idx, s_idx, slot_idx):
        out = []
        for r in range(pps):
            page_id = pt_ref[b_idx, s_idx * pps + r]
            out.append(pltpu.make_async_copy(ck_ref.at[layer, page_id], kbuf.at[slot_idx, r], sem.at[slot_idx, 0]))
            out.append(pltpu.make_async_copy(cv_ref.at[layer, page_id], vbuf.at[slot_idx, r], sem.at[slot_idx, 1]))
        return out

    @pl.when(flat == 0)
    def _():
        for c in page_copies(0, 0, 0):
            c.start()

    nxt = jnp.where(flat + 1 < total, flat + 1, 0)
    for c in page_copies(nxt // n_steps, nxt % n_steps, 1 - slot):
        c.start()
    for c in page_copies(pl.program_id(0), step, slot):
        c.wait()

    row_i = lax.broadcasted_iota(jnp.int32, (rows, e), 0)
    lane_i = lax.broadcasted_iota(jnp.int32, (rows, e), 1)
    row_c = row_i // half
    row_h = (row_i % half) // ts
    row_t = row_i % ts

    @pl.when(step == 0)
    def _():
        own_comp = (lane_i // head_dim) == row_c
        qb = jnp.zeros((rows, e), F32)
        for h in range(n_heads):
            for t in range(ts):
                qb = jnp.where(own_comp & (row_h == h) & (row_t == t), q_ref[t:t + 1, h * e:(h + 1) * e], qb)
        qb_ref[...] = qb
        m_ref[...] = jnp.full(m_ref.shape, -jnp.inf, F32)
        l_ref[...] = jnp.zeros(l_ref.shape, F32)
        acc_ref[...] = jnp.zeros(acc_ref.shape, F32)

    qb = qb_ref[...]
    qb16 = qb.astype(BF16)
    col_h = lax.broadcasted_iota(jnp.int32, (rows, page_rows), 1) % n_heads
    own_cols = col_h == (lax.broadcasted_iota(jnp.int32, (rows, page_rows), 0) % half) // ts
    s = [jnp.where(own_cols, _dot_nt(qb16, kbuf[slot, r].astype(BF16)), -jnp.inf) for r in range(pps)]
    m_old = m_ref[...]
    m_new = m_old
    for r in range(pps):
        m_new = jnp.maximum(m_new, jnp.max(s[r], axis=-1, keepdims=True))
    alpha = jnp.exp(m_old - m_new)
    l_new = alpha * l_ref[...]
    acc = alpha * acc_ref[...]
    for r in range(pps):
        p = jnp.exp(s[r] - m_new)
        l_new = l_new + jnp.sum(p, axis=-1, keepdims=True)
        acc = acc + _dot(p.astype(BF16), vbuf[slot, r].astype(BF16))
    m_ref[...] = m_new
    l_ref[...] = l_new
    acc_ref[...] = acc

    @pl.when(flat == total - 1)
    def _():
        for c in page_copies(0, 0, 1 - slot):
            c.wait()

    @pl.when(step == n_steps - 1)
    def _():
        m = m_ref[...]
        l = l_ref[...]
        a = acc_ref[...]

        def own_head_rows(ref, t):
            out = jnp.zeros((rows, e), F32)
            for h in range(n_heads):
                out = jnp.where(row_h == h, ref[t:t + 1, h * e:(h + 1) * e], out)
            return out

        row_t1 = row_t[:, 0:1]
        s_new = []
        for t in range(ts):
            s_t = jnp.sum(qb * own_head_rows(kn_ref, t), axis=-1, keepdims=True)
            s_new.append(jnp.where(row_t1 >= t, s_t, -jnp.inf))
        m_fin = m
        for t in range(ts):
            m_fin = jnp.maximum(m_fin, s_new[t])
        alpha_f = jnp.exp(m - m_fin)
        l = alpha_f * l
        a = alpha_f * a
        for t in range(ts):
            p_t = jnp.exp(s_new[t] - m_fin)
            l = l + p_t
            a = a + p_t * own_head_rows(vn_ref, t)
        d = a / l
        lam = _lambda_value(lp_ref[...], lam_init)
        o = d[0:half, :] - lam * d[half:rows, :]
        on = _rms_rows(o, sw_ref[...]) * (1.0 - lam_init)
        for h in range(n_heads):
            o_ref[:, h * e:(h + 1) * e] = on[h * ts:(h + 1) * ts, :].astype(o_ref.dtype)


def _attn_paged(q, k_new, v_new, cache_k, cache_v, page_table, attn_lambda, subln_w, layer, lam_init,
                *, n_heads, head_dim):
    db, ts, w = q.shape
    e = 2 * head_dim
    page_rows = cache_k.shape[2]
    n_pages = page_table.shape[1]
    pps = _pick_tile(n_pages, 16, 1)
    rows = 2 * n_heads * ts
    tok_spec = pl.BlockSpec((None, ts, w), lambda b, s, pt: (b, 0, 0))
    grid_spec = pltpu.PrefetchScalarGridSpec(
        num_scalar_prefetch=1,
        grid=(db, n_pages // pps),
        in_specs=[tok_spec, tok_spec, tok_spec,
                  pl.BlockSpec((None, 4, head_dim), lambda b, s, pt: (layer, 0, 0)),
                  pl.BlockSpec((None, 1, e), lambda b, s, pt: (layer, 0, 0)),
                  pl.BlockSpec(memory_space=pl.ANY), pl.BlockSpec(memory_space=pl.ANY)],
        out_specs=pl.BlockSpec((None, ts, w), lambda b, s, pt: (b, 0, 0)),
        scratch_shapes=[pltpu.VMEM((2, pps, page_rows, e), F32), pltpu.VMEM((2, pps, page_rows, e), F32),
                        pltpu.SemaphoreType.DMA((2, 2)),
                        pltpu.VMEM((rows, e), F32), pltpu.VMEM((rows, 1), F32),
                        pltpu.VMEM((rows, 1), F32), pltpu.VMEM((rows, e), F32)],
    )
    return pl.pallas_call(
        functools.partial(_attn_paged_kernel, layer=layer, pps=pps, n_heads=n_heads, head_dim=head_dim, ts=ts,
                          lam_init=lam_init),
        grid_spec=grid_spec,
        out_shape=jax.ShapeDtypeStruct((db, ts, w), BF16),
        compiler_params=_cparams("arbitrary", "arbitrary"),
        name="attn_paged",
    )(page_table, q, k_new, v_new, attn_lambda, subln_w, cache_k, cache_v)


def _gelu_tanh(x):
    return 0.5 * x * (1.0 + jnp.tanh(math.sqrt(2.0 / math.pi) * (x + 0.044715 * (x * x * x))))


def _conv_from_buffer(buf_ref, w_ref, *, halo, width, rows, row_block):
    first = halo - (width - 1)
    outs = []
    for r0 in range(0, rows, row_block):
        rb = min(row_block, rows - r0)
        acc = None
        for phase in range(min(SUBLANES, width)):
            taps = list(range(phase, width, SUBLANES))
            span = rb + (taps[-1] - phase)
            zs = buf_ref[pl.ds(first + r0 + phase, span), :]
            part = None
            for j in taps:
                term = w_ref[j:j + 1, :] * zs[j - phase:j - phase + rb, :]
                part = term if part is None else part + term
            acc = part if acc is None else acc + part
        outs.append(acc)
    return outs[0] if len(outs) == 1 else jnp.concatenate(outs, axis=0)


def _mixers_kernel(xb_ref, gb_ref, ca_ref, cb_ref, sb_ref, sc_ref, sx_ref,
                   h0_ref, lru_st_ref, cm_st_ref, sc_st_ref,
                   lcw_ref, lcb_ref, gw_ref, gbias_ref, ap_ref,
                   ccw_ref, ccb_ref, lnw_ref, lnb_ref, scw_ref,
                   y_ref, h_out_ref, lru_out_ref, cm_out_ref, sc_out_ref,
                   lru_buf, cm_buf, sc_buf, h_ref, hs_buf,
                   *, tt, n_t, halos, widths, blk):
    ti = pl.program_id(1)
    w = xb_ref.shape[-1]
    (halo_l, halo_c, halo_s), (wid_l, wid_c, wid_s) = halos, widths

    @pl.when(ti == 0)
    def _():
        lru_buf[0:halo_l, :] = lru_st_ref[...]
        cm_buf[0:halo_c, :] = cm_st_ref[...]
        sc_buf[0:halo_s, :] = sc_st_ref[...]
        h_ref[...] = h0_ref[...]

    lru_buf[halo_l:halo_l + tt, :] = xb_ref[...]
    xc = _conv_from_buffer(lru_buf, lcw_ref, halo=halo_l, width=wid_l, rows=tt, row_block=tt) + lcb_ref[...]
    xc16 = xc.astype(BF16)
    gates = []
    for g in range(2):
        parts = [_dot(xc16[:, n * blk:(n + 1) * blk], gw_ref[g, n].astype(BF16)) for n in range(w // blk)]
        gates.append(jnp.concatenate(parts, axis=1) + gbias_ref[g:g + 1, :])
    r_gate = jax.nn.sigmoid(gates[0])
    i_gate = jax.nn.sigmoid(gates[1])
    log_a = -LRU_C * r_gate * jax.nn.softplus(-ap_ref[...])
    a = jnp.exp(log_a)
    u = jnp.sqrt(1.0 - jnp.exp(2.0 * log_a)) * (i_gate * xc)
    if tt % SUBLANES == 0:
        groups = tt // SUBLANES
        a3 = a.reshape(groups, SUBLANES, w)
        u3 = u.reshape(groups, SUBLANES, w)
        sub = lax.broadcasted_iota(jnp.int32, (groups, SUBLANES, w), 1)
        dist = 1
        while dist < SUBLANES:
            live = sub >= dist
            a_prev = jnp.where(live, pltpu.roll(a3, dist, 1), 1.0)
            u_prev = jnp.where(live, pltpu.roll(u3, dist, 1), 0.0)
            u3 = a3 * u_prev + u3
            a3 = a3 * a_prev
            dist *= 2
        h_last = h_ref[...]
        blocks = []
        for g in range(groups):
            blk_h = u3[g] + a3[g] * h_last
            blocks.append(blk_h)
            h_last = blk_h[SUBLANES - 1:SUBLANES, :]
        hs = jnp.concatenate(blocks, axis=0)
    else:
        h_last = h_ref[...]
        for r in range(tt):
            h_last = a[r:r + 1, :] * h_last + u[r:r + 1, :]
            hs_buf[r:r + 1, :] = h_last
        hs = hs_buf[0:tt, :]
    h_ref[...] = h_last
    y_ref[:, 0:w] = (_gelu_tanh(gb_ref[...]) * hs).astype(y_ref.dtype)

    cm_buf[halo_c:halo_c + tt, :] = ca_ref[...] * jax.nn.sigmoid(cb_ref[...])
    uc = _conv_from_buffer(cm_buf, ccw_ref, halo=halo_c, width=wid_c, rows=tt, row_block=32) + ccb_ref[...]
    mu = jnp.mean(uc, axis=-1, keepdims=True)
    cen = uc - mu
    var = jnp.mean(cen * cen, axis=-1, keepdims=True)
    ln = cen * lax.rsqrt(var + NORM_EPS) * lnw_ref[...] + lnb_ref[...]
    y_ref[:, w:2 * w] = (ln * jax.nn.sigmoid(ln)).astype(y_ref.dtype)

    sc_buf[halo_s:halo_s + tt, :] = sc_ref[...] * sx_ref[...]
    sconv = _conv_from_buffer(sc_buf, scw_ref, halo=halo_s, width=wid_s, rows=tt, row_block=tt)
    y_ref[:, 2 * w:3 * w] = (sb_ref[...] * sconv).astype(y_ref.dtype)

    @pl.when(ti == n_t - 1)
    def _():
        h_out_ref[...] = h_last
        lru_out_ref[...] = lru_buf[pl.ds(tt + halo_l - (wid_l - 1), wid_l - 1), :]
        cm_out_ref[...] = cm_buf[pl.ds(tt + halo_c - (wid_c - 1), wid_c - 1), :]
        sc_out_ref[...] = sc_buf[pl.ds(tt + halo_s - (wid_s - 1), wid_s - 1), :]

    if n_t > 1:
        @pl.when(ti < n_t - 1)
        def _():
            lru_buf[0:halo_l, :] = lru_buf[tt:tt + halo_l, :]
            cm_buf[0:halo_c, :] = cm_buf[tt:tt + halo_c, :]
            sc_buf[0:halo_s, :] = sc_buf[tt:tt + halo_s, :]


def _pad_state(st, halo):
    return jnp.pad(st, ((0, 0), (halo - st.shape[1], 0), (0, 0)))


def _mixers(z, states, P, layer, *, w):
    b, t, _ = z.shape
    h0, lru_st, cm_st, sc_st = states
    widths = (P['lru_conv_w'].shape[1], P['cm_conv_w'].shape[1], P['sc_conv_w'].shape[1])
    halos = tuple(-(-(wd - 1) // SUBLANES) * SUBLANES for wd in widths)
    tt = _pick_tile(t, 256, SUBLANES)
    n_t = t // tt
    nb = P['lru_gate_w'].shape[2]
    blk = w // nb
    zspec = lambda slot: pl.BlockSpec((None, tt, w), lambda bi, ti: (bi, ti, slot))
    bspec = lambda rows: pl.BlockSpec((None, rows, w), lambda bi, ti: (bi, 0, 0))
    lspec = lambda rows: pl.BlockSpec((None, rows, w), lambda bi, ti: (layer, 0, 0))
    in_specs = [zspec(s) for s in range(3, 10)]
    in_specs += [bspec(1), bspec(halos[0]), bspec(halos[1]), bspec(halos[2])]
    in_specs += [lspec(widths[0]), lspec(1),
                 pl.BlockSpec((None, 2, nb, blk, blk), lambda bi, ti: (layer, 0, 0, 0, 0)),
                 lspec(2), lspec(1), lspec(widths[1]), lspec(1), lspec(1), lspec(1), lspec(widths[2])]
    out_specs = [pl.BlockSpec((None, tt, 3 * w), lambda bi, ti: (bi, ti, 0)),
                 bspec(1), bspec(widths[0] - 1), bspec(widths[1] - 1), bspec(widths[2] - 1)]
    out_shape = [jax.ShapeDtypeStruct((b, t, 3 * w), BF16),
                 jax.ShapeDtypeStruct((b, 1, w), F32),
                 jax.ShapeDtypeStruct((b, widths[0] - 1, w), F32),
                 jax.ShapeDtypeStruct((b, widths[1] - 1, w), F32),
                 jax.ShapeDtypeStruct((b, widths[2] - 1, w), F32)]
    row3 = lambda a: a.reshape(a.shape[0], 1, a.shape[1])
    return pl.pallas_call(
        functools.partial(_mixers_kernel, tt=tt, n_t=n_t, halos=halos, widths=widths, blk=blk),
        grid=(b, n_t),
        in_specs=in_specs,
        out_specs=out_specs,
        out_shape=out_shape,
        scratch_shapes=[pltpu.VMEM((halos[0] + tt, w), F32), pltpu.VMEM((halos[1] + tt, w), F32),
                        pltpu.VMEM((halos[2] + tt, w), F32), pltpu.VMEM((1, w), F32),
                        pltpu.VMEM((SUBLANES, w), F32)],
        compiler_params=_cparams("parallel", "arbitrary"),
        name="mixers",
    )(*([z] * 7), h0, _pad_state(lru_st, halos[0]), _pad_state(cm_st, halos[1]), _pad_state(sc_st, halos[2]),
      P['lru_conv_w'], row3(P['lru_conv_b']), P['lru_gate_w'], P['lru_gate_b'], row3(P['lru_a_param']),
      P['cm_conv_w'], row3(P['cm_conv_b']), row3(P['cm_ln_w']), row3(P['cm_ln_b']), P['sc_conv_w'])


def _layer(x, pos_tables, states, P, wb, layer, lam_init, dims, paged=None, cast_next=None):
    b, t, d = x.shape
    w, n_heads, head_dim = dims
    nw = P['norm_w']
    casting = 'wd1' not in wb
    wb = dict(wb)
    x2 = x.reshape(b * t, d)
    side = lambda *names: [(P[n], layer) for n in names] if casting else []
    h, cast = _norm_matmul(x2, nw, wb['gu1'], layer, 0, swiglu=True, out_dtype=BF16,
                           side=side('ffn1_w_down', 'w_in'))
    if casting:
        wb['wd1'], wb['w_in'] = cast
    x2, cast = _matmul_norm_res([h], wb['wd1'], x2, nw, layer, 1, 0.5, side=side('ffn2_w_gu'))
    if casting:
        wb['gu2'], = cast
    z, cast = _norm_matmul(x2, nw, wb['w_in'], layer, 2, swiglu=False, out_dtype=F32, tn_target=1024,
                           side=side('w_out'))
    if casting:
        wb['w_out'], = cast
    z = z.reshape(b, t, N_IN_SLOTS * w)
    cos, sin = pos_tables
    e = 2 * head_dim
    q, k_rows, v_rows, k16, v16 = _rope(z, cos, sin, w=w, n_heads=n_heads, head_dim=head_dim,
                                        q_dtype=BF16 if paged is None else F32)
    if paged is None:
        ya = _attn_prompt(q, k16, v16, P['attn_lambda'], P['attn_subln_w'], layer, lam_init,
                          n_heads=n_heads, head_dim=head_dim)
    else:
        cache_k, cache_v, page_table = paged
        ya = _attn_paged(q, k_rows.reshape(b, t, w), v_rows.reshape(b, t, w), cache_k, cache_v, page_table,
                         P['attn_lambda'], P['attn_subln_w'], layer, lam_init, n_heads=n_heads, head_dim=head_dim)
    ybcd, h_t, lru_buf, cm_buf, sc_buf = _mixers(z, states, P, layer, w=w)
    x2, _ = _matmul_norm_res([ya.reshape(b * t, w), ybcd.reshape(b * t, 3 * w)], wb['w_out'], x2, nw, layer, 3, 1.0)
    h, cast = _norm_matmul(x2, nw, wb['gu2'], layer, 4, swiglu=True, out_dtype=BF16, side=side('ffn2_w_down'))
    if casting:
        wb['wd2'], = cast
    next_side = [(P['ffn1_w_gu'], cast_next)] if casting and cast_next is not None else []
    x2, cast = _matmul_norm_res([h], wb['wd2'], x2, nw, layer, 5, 0.5, side=next_side)
    if next_side:
        wb['next_gu1'], = cast
    outs = (k_rows.reshape(b, t, n_heads, e), v_rows.reshape(b, t, n_heads, e), h_t.reshape(b, w),
            lru_buf, cm_buf, sc_buf)
    return x2.reshape(b, t, d), outs, wb


def kernel(x_prompt, x_sample, cache_k, cache_v, state_lru_h, state_lru_conv, state_cm_conv, state_sc_conv, page_table, norm_w, ffn1_w_gu, ffn1_w_down, ffn2_w_gu, ffn2_w_down, w_in, w_out, attn_lambda, attn_subln_w, lru_conv_w, lru_conv_b, lru_gate_w, lru_gate_b, lru_a_param, cm_conv_w, cm_conv_b, cm_ln_w, cm_ln_b, sc_conv_w):
    b, t, d = x_prompt.shape
    db, ts, _ = x_sample.shape
    depth = norm_w.shape[0]
    w = d // N_GROUPS
    n_heads = cache_k.shape[3]
    e = cache_k.shape[4]
    head_dim = e // 2
    assert n_heads * e == w and w_in.shape[2] == N_IN_SLOTS * w
    n_pool, page = cache_k.shape[1], cache_k.shape[2]
    past = page_table.shape[1] * page
    dims = (w, n_heads, head_dim)

    P = {
        'norm_w': norm_w,
        'ffn1_w_gu': ffn1_w_gu, 'ffn1_w_down': ffn1_w_down, 'ffn2_w_gu': ffn2_w_gu,
        'ffn2_w_down': ffn2_w_down, 'w_in': w_in, 'w_out': w_out,
        'attn_lambda': attn_lambda,
        'attn_subln_w': attn_subln_w.reshape(depth, 1, e),
        'lru_conv_w': lru_conv_w, 'lru_conv_b': lru_conv_b, 'lru_gate_w': lru_gate_w,
        'lru_gate_b': lru_gate_b, 'lru_a_param': lru_a_param,
        'cm_conv_w': cm_conv_w, 'cm_conv_b': cm_conv_b, 'cm_ln_w': cm_ln_w, 'cm_ln_b': cm_ln_b,
        'sc_conv_w': sc_conv_w,
    }
    cache_k4 = cache_k.reshape(depth, n_pool, page * n_heads, e)
    cache_v4 = cache_v.reshape(depth, n_pool, page * n_heads, e)
    tables_p = _rope_tables(jnp.arange(t), w, head_dim)
    tables_s = _rope_tables(past + jnp.arange(ts), w, head_dim)
    zeros_p = (jnp.zeros((b, 1, w), F32), jnp.zeros((b, lru_conv_w.shape[1] - 1, w), F32),
               jnp.zeros((b, cm_conv_w.shape[1] - 1, w), F32), jnp.zeros((b, sc_conv_w.shape[1] - 1, w), F32))

    xp, xs = x_prompt, x_sample
    outs_p, outs_s = [], []
    gu1 = ffn1_w_gu[0].astype(BF16)
    for l in range(depth):
        lam_init = 0.8 - 0.6 * math.exp(-0.3 * l)
        xp, sp, wb = _layer(xp, tables_p, zeros_p, P, {'gu1': gu1}, l, lam_init, dims,
                            cast_next=l + 1 if l + 1 < depth else None)
        outs_p.append(sp)
        st_s = (state_lru_h[l].reshape(db, 1, w), state_lru_conv[l], state_cm_conv[l], state_sc_conv[l])
        xs, ss, _ = _layer(xs, tables_s, st_s, P, wb, l, lam_init, dims, paged=(cache_k4, cache_v4, page_table))
        outs_s.append(ss)
        gu1 = wb.get('next_gu1')
    stack = lambda outs, i: jnp.stack([o[i] for o in outs])
    return (xp, xs) + tuple(stack(outs_p, i) for i in range(6)) + tuple(stack(outs_s, i) for i in range(6))
```

```python
import functools
import math

import jax
import jax.numpy as jnp
from jax import lax
from jax.experimental import pallas as pl
from jax.experimental.pallas import tpu as pltpu

F32 = jnp.float32
BF16 = jnp.bfloat16

NORM_EPS = 1e-6
ROPE_THETA = 10000.0
LRU_C = 8.0
N_GROUPS = 4
N_IN_SLOTS = 10
VMEM_LIMIT_BYTES = 56 * 1024 * 1024
LANES = 128
SUBLANES = 8


def _cparams(*sem):
    return pltpu.CompilerParams(dimension_semantics=sem, vmem_limit_bytes=VMEM_LIMIT_BYTES)


def _pick_tile(n, target, quantum):
    if n <= target:
        return n
    t = (target // quantum) * quantum
    while t >= quantum:
        if n % t == 0:
            return t
        t -= quantum
    return n


def _dot(a, b):
    return jnp.dot(a, b, preferred_element_type=F32)


def _dot_nt(a, b):
    return lax.dot_general(a, b, (((1,), (1,)), ((), ())), preferred_element_type=F32)


def _rms_rows(x, w_row):
    ms = jnp.mean(x * x, axis=-1, keepdims=True)
    return x * lax.rsqrt(ms + NORM_EPS) * w_row


def _side_cast_plan(srcs, n_steps, step_index):
    in_specs, out_specs, out_shapes = [], [], []
    for src, layer in srcs:
        _, r, c = src.shape
        nb = next(n for n in range(min(n_steps, r // 16), 0, -1) if r % n == 0 and (r // n) % 16 == 0)
        rb = r // nb

        def in_map(*ids, layer=layer, nb=nb):
            return (layer, jnp.minimum(step_index(*ids), nb - 1), 0)

        def out_map(*ids, nb=nb):
            return (jnp.minimum(step_index(*ids), nb - 1), 0)

        in_specs.append(pl.BlockSpec((None, rb, c), in_map))
        out_specs.append(pl.BlockSpec((rb, c), out_map))
        out_shapes.append(jax.ShapeDtypeStruct((r, c), BF16))
    return in_specs, out_specs, out_shapes


def _run_side_casts(src_refs, dst_refs):
    for src_ref, dst_ref in zip(src_refs, dst_refs):
        dst_ref[...] = src_ref[...].astype(BF16)


def _norm_matmul_kernel(x_ref, xr_ref, nw_ref, *refs, norm_row, swiglu, n_side, tm):
    n_w = 2 if swiglu else 1
    w_refs = refs[:n_w]
    side_src = refs[n_w:n_w + n_side]
    out_ref, outr_ref = refs[n_w + n_side:n_w + n_side + 2]
    side_dst = refs[n_w + n_side + 2:n_w + 2 * n_side + 2]
    xn_ref = refs[-1]
    i = pl.program_id(0)
    j = pl.program_id(1)
    rows_all = xn_ref.shape[0]
    nw_row = nw_ref[norm_row:norm_row + 1, :]

    @pl.when(j == 0)
    def _():
        xn_ref[0:tm, :] = _rms_rows(x_ref[...], nw_row).astype(BF16)

    @pl.when((j == 0) & (i == 0))
    def _():
        xn_ref[tm:rows_all, :] = _rms_rows(xr_ref[...], nw_row).astype(BF16)

    def product(rows):
        xn = xn_ref[0:rows, :]
        if swiglu:
            g = _dot(xn, w_refs[0][...])
            u = _dot(xn, w_refs[1][...])
            return g * jax.nn.sigmoid(g) * u
        return _dot(xn, w_refs[0][...])

    @pl.when(i == 0)
    def _():
        r = product(rows_all)
        out_ref[...] = r[0:tm, :].astype(out_ref.dtype)
        outr_ref[...] = r[tm:rows_all, :].astype(outr_ref.dtype)
        _run_side_casts(side_src, side_dst)

    @pl.when(i != 0)
    def _():
        out_ref[...] = product(tm).astype(out_ref.dtype)
        _run_side_casts(side_src, side_dst)


def _norm_matmul(x, x_rider, norm_w, w, layer, norm_row, *, swiglu, out_dtype, tn_target=512, side=()):
    m, d = x.shape
    mr = x_rider.shape[0]
    n = w.shape[1]
    n_out = n // 2 if swiglu else n
    tm = _pick_tile(m, 1024, 16)
    tn = _pick_tile(n_out, tn_target, LANES)
    n_blocks = n_out // tn
    grid = (m // tm, n_blocks)
    in_specs = [
        pl.BlockSpec((tm, d), lambda i, j: (i, 0)),
        pl.BlockSpec((mr, d), lambda i, j: (0, 0)),
        pl.BlockSpec((None, norm_w.shape[1], d), lambda i, j: (layer, 0, 0)),
        pl.BlockSpec((d, tn), lambda i, j: (0, j)),
    ]
    args = [x, x_rider, norm_w, w]
    if swiglu:
        in_specs.append(pl.BlockSpec((d, tn), lambda i, j: (0, j + n_blocks)))
        args.append(w)
    s_in, s_out, s_shapes = _side_cast_plan(side, grid[0] * grid[1], lambda i, j: i * n_blocks + j)
    rider_spec = pl.BlockSpec((mr, tn), lambda i, j: (0, jnp.where(i == 0, j, n_blocks - 1)))
    outs = pl.pallas_call(
        functools.partial(_norm_matmul_kernel, norm_row=norm_row, swiglu=swiglu, n_side=len(side), tm=tm),
        grid=grid,
        in_specs=in_specs + s_in,
        out_specs=[pl.BlockSpec((tm, tn), lambda i, j: (i, j)), rider_spec] + s_out,
        out_shape=[jax.ShapeDtypeStruct((m, n_out), out_dtype), jax.ShapeDtypeStruct((mr, n_out), out_dtype)]
                  + s_shapes,
        scratch_shapes=[pltpu.VMEM((tm + mr, d), BF16)],
        compiler_params=_cparams("arbitrary", "arbitrary"),
        name="norm_matmul_swiglu" if swiglu else "norm_matmul",
    )(*args, *[src for src, _ in side])
    return outs[0], outs[1], outs[2:]


def _matmul_norm_res_kernel(*refs, n_parts, part_rows, norm_row, scale, n_side, tm):
    h_refs = refs[:n_parts]
    hr_refs = refs[n_parts:2 * n_parts]
    w_ref, x_ref, xr_ref, nw_ref = refs[2 * n_parts:2 * n_parts + 4]
    base = 2 * n_parts + 4
    side_src = refs[base:base + n_side]
    out_ref, outr_ref = refs[base + n_side:base + n_side + 2]
    side_dst = refs[base + n_side + 2:base + 2 * n_side + 2]
    lhs_refs = refs[base + 2 * n_side + 2:]
    i = pl.program_id(0)
    nw_row = nw_ref[norm_row:norm_row + 1, :]

    def product(lhs_list):
        y = None
        row = 0
        for lhs, rows in zip(lhs_list, part_rows):
            part = _dot(lhs, w_ref[row:row + rows, :])
            y = part if y is None else y + part
            row += rows
        return _rms_rows(y, nw_row)

    @pl.when(i == 0)
    def _():
        for h_ref, hr_ref, lhs_ref in zip(h_refs, hr_refs, lhs_refs):
            lhs_ref[0:tm, :] = h_ref[...]
            lhs_ref[tm:lhs_ref.shape[0], :] = hr_ref[...]
        yn = product([lhs_ref[...] for lhs_ref in lhs_refs])
        out_ref[...] = x_ref[...] + scale * yn[0:tm, :]
        outr_ref[...] = xr_ref[...] + scale * yn[tm:yn.shape[0], :]
        _run_side_casts(side_src, side_dst)

    @pl.when(i != 0)
    def _():
        out_ref[...] = x_ref[...] + scale * product([h_ref[...] for h_ref in h_refs])
        _run_side_casts(side_src, side_dst)


def _matmul_norm_res(h_parts, h_rider_parts, w, x, x_rider, norm_w, layer, norm_row, scale, side=()):
    m, d = x.shape
    mr = x_rider.shape[0]
    k = w.shape[0]
    part_rows = tuple(h.shape[1] for h in h_parts)
    assert sum(part_rows) == k
    tm = _pick_tile(m, 256 if k > 4096 else 512, 16)
    in_specs = [pl.BlockSpec((tm, r), lambda i: (i, 0)) for r in part_rows]
    in_specs += [pl.BlockSpec((mr, r), lambda i: (0, 0)) for r in part_rows]
    in_specs += [
        pl.BlockSpec((k, d), lambda i: (0, 0), pipeline_mode=pl.Buffered(1)),
        pl.BlockSpec((tm, d), lambda i: (i, 0)),
        pl.BlockSpec((mr, d), lambda i: (0, 0)),
        pl.BlockSpec((None, norm_w.shape[1], d), lambda i: (layer, 0, 0)),
    ]
    s_in, s_out, s_shapes = _side_cast_plan(side, m // tm, lambda i: i)
    outs = pl.pallas_call(
        functools.partial(_matmul_norm_res_kernel, n_parts=len(h_parts), part_rows=part_rows,
                          norm_row=norm_row, scale=scale, n_side=len(side), tm=tm),
        grid=(m // tm,),
        in_specs=in_specs + s_in,
        out_specs=[pl.BlockSpec((tm, d), lambda i: (i, 0)), pl.BlockSpec((mr, d), lambda i: (0, 0))] + s_out,
        out_shape=[jax.ShapeDtypeStruct((m, d), F32), jax.ShapeDtypeStruct((mr, d), F32)] + s_shapes,
        scratch_shapes=[pltpu.VMEM((tm + mr, r), BF16) for r in part_rows],
        compiler_params=_cparams("arbitrary"),
        name="matmul_norm_res",
    )(*h_parts, *h_rider_parts, w, x, x_rider, norm_w, *[src for src, _ in side])
    return outs[0], outs[1], outs[2:]


def _swap_halves(x, half):
    n = x.shape[-1]
    lane = lax.broadcasted_iota(jnp.int32, x.shape, x.ndim - 1)
    first = (lane & half) == 0
    return jnp.where(first, pltpu.roll(x, n - half, x.ndim - 1), pltpu.roll(x, half, x.ndim - 1))


def _rope_kernel(q_ref, k_ref, v_ref, cos_ref, sin_ref, qo_ref, ko_ref, vo_ref, kb_ref, vb_ref,
                 *, half, q_scale, n_heads):
    cos = cos_ref[...]
    sin = sin_ref[...]
    q = q_ref[...]
    k = k_ref[...]
    v = v_ref[...]
    qr = q * cos + _swap_halves(q, half) * sin
    kr = k * cos + _swap_halves(k, half) * sin
    qo_ref[...] = (qr * q_scale).astype(qo_ref.dtype)
    kb_ref[...] = kr.astype(BF16)
    vb_ref[...] = v.astype(BF16)
    tt = k.shape[0]
    e = k.shape[1] // n_heads
    for h in range(n_heads):
        ko_ref[pl.ds(h, tt, stride=n_heads), :] = kr[:, h * e:(h + 1) * e]
        vo_ref[pl.ds(h, tt, stride=n_heads), :] = v[:, h * e:(h + 1) * e]


def _rope(z, cos, sin, *, w, n_heads, head_dim, q_dtype):
    b, t, _ = z.shape
    e = w // n_heads
    tt = _pick_tile(t, 512, 16)
    zspec = lambda slot: pl.BlockSpec((None, tt, w), lambda bi, ti: (bi, ti, slot))
    tspec = pl.BlockSpec((tt, w), lambda bi, ti: (ti, 0))
    ospec = pl.BlockSpec((None, tt, w), lambda bi, ti: (bi, ti, 0))
    rspec = pl.BlockSpec((None, tt * n_heads, e), lambda bi, ti: (bi, ti, 0))
    rows_shape = jax.ShapeDtypeStruct((b, t * n_heads, e), F32)
    return pl.pallas_call(
        functools.partial(_rope_kernel, half=head_dim // 2, q_scale=head_dim ** -0.5, n_heads=n_heads),
        grid=(b, t // tt),
        in_specs=[zspec(0), zspec(1), zspec(2), tspec, tspec],
        out_specs=[ospec, rspec, rspec, ospec, ospec],
        out_shape=[jax.ShapeDtypeStruct((b, t, w), q_dtype), rows_shape, rows_shape,
                   jax.ShapeDtypeStruct((b, t, w), BF16), jax.ShapeDtypeStruct((b, t, w), BF16)],
        compiler_params=_cparams("parallel", "parallel"),
        name="rope",
    )(z, z, z, cos, sin)


def _rope_tables(pos, w, head_dim):
    half = head_dim // 2
    inv = ROPE_THETA ** (-jnp.arange(half, dtype=F32) / half)
    ang = pos.astype(F32)[:, None] * inv[None, :]
    cos = jnp.cos(ang)
    sin = jnp.sin(ang)
    cos_d = jnp.concatenate([cos, cos], axis=-1)
    sin_d = jnp.concatenate([-sin, sin], axis=-1)
    reps = w // head_dim
    return jnp.tile(cos_d, (1, reps)), jnp.tile(sin_d, (1, reps))


def _lambda_value(lp, lam_init):
    s1 = jnp.sum(lp[0:1, :] * lp[1:2, :], axis=-1, keepdims=True)
    s2 = jnp.sum(lp[2:3, :] * lp[3:4, :], axis=-1, keepdims=True)
    return jnp.exp(s1) - jnp.exp(s2) + lam_init


def _attn_prompt_kernel(q_ref, k_ref, v_ref, lp_ref, sw_ref, o_ref, *, tile, head_dim, lam_init):
    t, e = q_ref.shape
    lam = _lambda_value(lp_ref[...], lam_init)
    lane = lax.broadcasted_iota(jnp.int32, (tile, e), 1)
    row = lax.broadcasted_iota(jnp.int32, (tile, tile), 0)
    col = lax.broadcasted_iota(jnp.int32, (tile, tile), 1)
    keep = col <= row
    for i in range(t // tile):
        kv = (i + 1) * tile
        q = q_ref[i * tile:kv, :]
        ks = k_ref[0:kv, :]
        vx = jnp.concatenate([v_ref[0:kv, :], jnp.ones((kv, e), BF16)], axis=1)
        normed = []
        for c in range(2):
            qc = jnp.where((lane >= c * head_dim) & (lane < (c + 1) * head_dim), q, jnp.zeros_like(q))
            s = _dot_nt(qc, ks)
            s_diag = jnp.where(keep, s[:, i * tile:], -jnp.inf)
            m = jnp.max(s_diag, axis=-1, keepdims=True)
            if i > 0:
                s_full = s[:, :i * tile]
                m = jnp.maximum(m, jnp.max(s_full, axis=-1, keepdims=True))
                p = jnp.concatenate([jnp.exp(s_full - m), jnp.exp(s_diag - m)], axis=1)
            else:
                p = jnp.exp(s_diag - m)
            acc = _dot(p.astype(BF16), vx)
            normed.append(acc[:, :e] / acc[:, e:])
        o = normed[0] - lam * normed[1]
        o_ref[i * tile:kv, :] = (_rms_rows(o, sw_ref[...]) * (1.0 - lam_init)).astype(o_ref.dtype)


def _attn_prompt(q, k, v, attn_lambda, subln_w, layer, lam_init, *, n_heads, head_dim):
    b, t, w = q.shape
    e = 2 * head_dim
    tile = _pick_tile(t, 256, LANES)
    head_spec = pl.BlockSpec((None, t, e), lambda bi, h: (bi, 0, h))
    return pl.pallas_call(
        functools.partial(_attn_prompt_kernel, tile=tile, head_dim=head_dim, lam_init=lam_init),
        grid=(b, n_heads),
        in_specs=[
            head_spec, head_spec, head_spec,
            pl.BlockSpec((None, 4, head_dim), lambda bi, h: (layer, 0, 0)),
            pl.BlockSpec((None, 1, e), lambda bi, h: (layer, 0, 0)),
        ],
        out_specs=head_spec,
        out_shape=jax.ShapeDtypeStruct((b, t, w), BF16),
        compiler_params=_cparams("parallel", "parallel"),
        name="attn_prompt",
    )(q, k, v, attn_lambda, subln_w)


def _attn_paged_kernel(pt_ref, q_ref, kn_ref, vn_ref, lp_ref, sw_ref, ck_ref, cv_ref, o_ref,
                       kbuf, vbuf, sem, qb_ref, m_ref, l_ref, acc_ref,
                       *, layer, pps, n_heads, head_dim, ts, lam_init):
    step = pl.program_id(1)
    n_steps = pl.num_programs(1)
    flat = pl.program_id(0) * n_steps + step
    total = pl.num_programs(0) * n_steps
    slot = flat % 2
    e = 2 * head_dim
    half = n_heads * ts
    rows = 2 * half
    page_rows = kbuf.shape[2]

    def page_copies(b_idx, s_idx, slot_idx):
        out = []
        for r in range(pps):
            page_id = pt_ref[b_idx, s_idx * pps + r]
            out.append(pltpu.make_async_copy(ck_ref.at[layer, page_id], kbuf.at[slot_idx, r], sem.at[slot_idx, 0]))
            out.append(pltpu.make_async_copy(cv_ref.at[layer, page_id], vbuf.at[slot_idx, r], sem.at[slot_idx, 1]))
        return out

    @pl.when(flat == 0)
    def _():
        for c in page_copies(0, 0, 0):
            c.start()

    nxt = jnp.where(flat + 1 < total, flat + 1, 0)
    for c in page_copies(nxt // n_steps, nxt % n_steps, 1 - slot):
        c.start()
    for c in page_copies(pl.program_id(0), step, slot):
        c.wait()

    row_i = lax.broadcasted_iota(jnp.int32, (rows, e), 0)
    lane_i = lax.broadcasted_iota(jnp.int32, (rows, e), 1)
    row_c = row_i // half
    row_h = (row_i % half) // ts
    row_t = row_i % ts

    @pl.when(step == 0)
    def _():
        own_comp = (lane_i // head_dim) == row_c
        qb = jnp.zeros((rows, e), F32)
        for h in range(n_heads):
            for t in range(ts):
                qb = jnp.where(own_comp & (row_h == h) & (row_t == t), q_ref[t:t + 1, h * e:(h + 1) * e], qb)
        qb_ref[...] = qb
        m_ref[...] = jnp.full(m_ref.shape, -jnp.inf, F32)
        l_ref[...] = jnp.zeros(l_ref.shape, F32)
        acc_ref[...] = jnp.zeros(acc_ref.shape, F32)

    qb = qb_ref[...]
    qb16 = qb.astype(BF16)
    col_h = lax.broadcasted_iota(jnp.int32, (rows, page_rows), 1) % n_heads
    own_cols = col_h == (lax.broadcasted_iota(jnp.int32, (rows, page_rows), 0) % half) // ts
    s = [jnp.where(own_cols, _dot_nt(qb16, kbuf[slot, r].astype(BF16)), -jnp.inf) for r in range(pps)]
    m_old = m_ref[...]
    m_new = m_old
    for r in range(pps):
        m_new = jnp.maximum(m_new, jnp.max(s[r], axis=-1, keepdims=True))
    alpha = jnp.exp(m_old - m_new)
    l_new = alpha * l_ref[...]
    acc = alpha * acc_ref[...]
    for r in range(pps):
        p = jnp.exp(s[r] - m_new)
        l_new = l_new + jnp.sum(p, axis=-1, keepdims=True)
        acc = acc + _dot(p.astype(BF16), vbuf[slot, r].astype(BF16))
    m_ref[...] = m_new
    l_ref[...] = l_new
    acc_ref[...] = acc

    @pl.when(flat == total - 1)
    def _():
        for c in page_copies(0, 0, 1 - slot):
            c.wait()

    @pl.when(step == n_steps - 1)
    def _():
        m = m_ref[...]
        l = l_ref[...]
        a = acc_ref[...]

        def own_head_rows(ref, t):
            out = jnp.zeros((rows, e), F32)
            for h in range(n_heads):
                out = jnp.where(row_h == h, ref[t:t + 1, h * e:(h + 1) * e], out)
            return out

        row_t1 = row_t[:, 0:1]
        s_new = []
        for t in range(ts):
            s_t = jnp.sum(qb * own_head_rows(kn_ref, t), axis=-1, keepdims=True)
            s_new.append(jnp.where(row_t1 >= t, s_t, -jnp.inf))
        m_fin = m
        for t in range(ts):
            m_fin = jnp.maximum(m_fin, s_new[t])
        alpha_f = jnp.exp(m - m_fin)
        l = alpha_f * l
        a = alpha_f * a
        for t in range(ts):
            p_t = jnp.exp(s_new[t] - m_fin)
            l = l + p_t
            a = a + p_t * own_head_rows(vn_ref, t)
        d = a / l
        lam = _lambda_value(lp_ref[...], lam_init)
        o = d[0:half, :] - lam * d[half:rows, :]
        on = _rms_rows(o, sw_ref[...]) * (1.0 - lam_init)
        for h in range(n_heads):
            o_ref[:, h * e:(h + 1) * e] = on[h * ts:(h + 1) * ts, :].astype(o_ref.dtype)


def _attn_paged(q, k_new, v_new, cache_k, cache_v, page_table, attn_lambda, subln_w, layer, lam_init,
                *, n_heads, head_dim):
    db, ts, w = q.shape
    e = 2 * head_dim
    page_rows = cache_k.shape[2]
    n_pages = page_table.shape[1]
    pps = _pick_tile(n_pages, 16, 1)
    rows = 2 * n_heads * ts
    tok_spec = pl.BlockSpec((None, ts, w), lambda b, s, pt: (b, 0, 0))
    grid_spec = pltpu.PrefetchScalarGridSpec(
        num_scalar_prefetch=1,
        grid=(db, n_pages // pps),
        in_specs=[tok_spec, tok_spec, tok_spec,
                  pl.BlockSpec((None, 4, head_dim), lambda b, s, pt: (layer, 0, 0)),
                  pl.BlockSpec((None, 1, e), lambda b, s, pt: (layer, 0, 0)),
                  pl.BlockSpec(memory_space=pl.ANY), pl.BlockSpec(memory_space=pl.ANY)],
        out_specs=pl.BlockSpec((None, ts, w), lambda b, s, pt: (b, 0, 0)),
        scratch_shapes=[pltpu.VMEM((2, pps, page_rows, e), F32), pltpu.VMEM((2, pps, page_rows, e), F32),
                        pltpu.SemaphoreType.DMA((2, 2)),
                        pltpu.VMEM((rows, e), F32), pltpu.VMEM((rows, 1), F32),
                        pltpu.VMEM((rows, 1), F32), pltpu.VMEM((rows, e), F32)],
    )
    return pl.pallas_call(
        functools.partial(_attn_paged_kernel, layer=layer, pps=pps, n_heads=n_heads, head_dim=head_dim, ts=ts,
                          lam_init=lam_init),
        grid_spec=grid_spec,
        out_shape=jax.ShapeDtypeStruct((db, ts, w), BF16),
        compiler_params=_cparams("arbitrary", "arbitrary"),
        name="attn_paged",
    )(page_table, q, k_new, v_new, attn_lambda, subln_w, cache_k, cache_v)


def _gelu_tanh(x):
    return 0.5 * x * (1.0 + jnp.tanh(math.sqrt(2.0 / math.pi) * (x + 0.044715 * (x * x * x))))


def _conv_from_buffer(buf_ref, w_ref, *, halo, width, rows, row_block):
    first = halo - (width - 1)
    outs = []
    for r0 in range(0, rows, row_block):
        rb = min(row_block, rows - r0)
        acc = None
        for phase in range(min(SUBLANES, width)):
            taps = list(range(phase, width, SUBLANES))
            span = rb + (taps[-1] - phase)
            zs = buf_ref[pl.ds(first + r0 + phase, span), :]
            part = None
            for j in taps:
                term = w_ref[j:j + 1, :] * zs[j - phase:j - phase + rb, :]
                part = term if part is None else part + term
            acc = part if acc is None else acc + part
        outs.append(acc)
    return outs[0] if len(outs) == 1 else jnp.concatenate(outs, axis=0)


def _mixers_kernel(xb_ref, gb_ref, ca_ref, cb_ref, sb_ref, sc_ref, sx_ref,
                   h0_ref, lru_st_ref, cm_st_ref, sc_st_ref,
                   lcw_ref, lcb_ref, gw_ref, gbias_ref, ap_ref,
                   ccw_ref, ccb_ref, lnw_ref, lnb_ref, scw_ref,
                   y_ref, h_out_ref, lru_out_ref, cm_out_ref, sc_out_ref,
                   lru_buf, cm_buf, sc_buf, h_ref, hs_buf,
                   *, tt, n_t, halos, widths, blk):
    ti = pl.program_id(1)
    w = xb_ref.shape[-1]
    (halo_l, halo_c, halo_s), (wid_l, wid_c, wid_s) = halos, widths

    @pl.when(ti == 0)
    def _():
        lru_buf[0:halo_l, :] = lru_st_ref[...]
        cm_buf[0:halo_c, :] = cm_st_ref[...]
        sc_buf[0:halo_s, :] = sc_st_ref[...]
        h_ref[...] = h0_ref[...]

    lru_buf[halo_l:halo_l + tt, :] = xb_ref[...]
    xc = _conv_from_buffer(lru_buf, lcw_ref, halo=halo_l, width=wid_l, rows=tt, row_block=tt) + lcb_ref[...]
    xc16 = xc.astype(BF16)
    gates = []
    for g in range(2):
        parts = [_dot(xc16[:, n * blk:(n + 1) * blk], gw_ref[g, n].astype(BF16)) for n in range(w // blk)]
        gates.append(jnp.concatenate(parts, axis=1) + gbias_ref[g:g + 1, :])
    r_gate = jax.nn.sigmoid(gates[0])
    i_gate = jax.nn.sigmoid(gates[1])
    log_a = -LRU_C * r_gate * jax.nn.softplus(-ap_ref[...])
    a = jnp.exp(log_a)
    u = jnp.sqrt(1.0 - jnp.exp(2.0 * log_a)) * (i_gate * xc)
    if tt % SUBLANES == 0:
        groups = tt // SUBLANES
        a3 = a.reshape(groups, SUBLANES, w)
        u3 = u.reshape(groups, SUBLANES, w)
        sub = lax.broadcasted_iota(jnp.int32, (groups, SUBLANES, w), 1)
        dist = 1
        while dist < SUBLANES:
            live = sub >= dist
            a_prev = jnp.where(live, pltpu.roll(a3, dist, 1), 1.0)
            u_prev = jnp.where(live, pltpu.roll(u3, dist, 1), 0.0)
            u3 = a3 * u_prev + u3
            a3 = a3 * a_prev
            dist *= 2
        h_last = h_ref[...]
        blocks = []
        for g in range(groups):
            blk_h = u3[g] + a3[g] * h_last
            blocks.append(blk_h)
            h_last = blk_h[SUBLANES - 1:SUBLANES, :]
        hs = jnp.concatenate(blocks, axis=0)
    else:
        h_last = h_ref[...]
        for r in range(tt):
            h_last = a[r:r + 1, :] * h_last + u[r:r + 1, :]
            hs_buf[r:r + 1, :] = h_last
        hs = hs_buf[0:tt, :]
    h_ref[...] = h_last
    y_ref[:, 0:w] = (_gelu_tanh(gb_ref[...]) * hs).astype(y_ref.dtype)

    cm_buf[halo_c:halo_c + tt, :] = ca_ref[...] * jax.nn.sigmoid(cb_ref[...])
    uc = _conv_from_buffer(cm_buf, ccw_ref, halo=halo_c, width=wid_c, rows=tt, row_block=32) + ccb_ref[...]
    mu = jnp.mean(uc, axis=-1, keepdims=True)
    cen = uc - mu
    var = jnp.mean(cen * cen, axis=-1, keepdims=True)
    ln = cen * lax.rsqrt(var + NORM_EPS) * lnw_ref[...] + lnb_ref[...]
    y_ref[:, w:2 * w] = (ln * jax.nn.sigmoid(ln)).astype(y_ref.dtype)

    sc_buf[halo_s:halo_s + tt, :] = sc_ref[...] * sx_ref[...]
    sconv = _conv_from_buffer(sc_buf, scw_ref, halo=halo_s, width=wid_s, rows=tt, row_block=tt)
    y_ref[:, 2 * w:3 * w] = (sb_ref[...] * sconv).astype(y_ref.dtype)

    @pl.when(ti == n_t - 1)
    def _():
        h_out_ref[...] = h_last
        lru_out_ref[...] = lru_buf[pl.ds(tt + halo_l - (wid_l - 1), wid_l - 1), :]
        cm_out_ref[...] = cm_buf[pl.ds(tt + halo_c - (wid_c - 1), wid_c - 1), :]
        sc_out_ref[...] = sc_buf[pl.ds(tt + halo_s - (wid_s - 1), wid_s - 1), :]

    if n_t > 1:
        @pl.when(ti < n_t - 1)
        def _():
            lru_buf[0:halo_l, :] = lru_buf[tt:tt + halo_l, :]
            cm_buf[0:halo_c, :] = cm_buf[tt:tt + halo_c, :]
            sc_buf[0:halo_s, :] = sc_buf[tt:tt + halo_s, :]


def _pad_state(st, halo):
    return jnp.pad(st, ((0, 0), (halo - st.shape[1], 0), (0, 0)))


def _mixers(z, states, P, layer, *, w):
    b, t, _ = z.shape
    h0, lru_st, cm_st, sc_st = states
    widths = (P['lru_conv_w'].shape[1], P['cm_conv_w'].shape[1], P['sc_conv_w'].shape[1])
    halos = tuple(-(-(wd - 1) // SUBLANES) * SUBLANES for wd in widths)
    tt = _pick_tile(t, 256, SUBLANES)
    n_t = t // tt
    nb = P['lru_gate_w'].shape[2]
    blk = w // nb
    zspec = lambda slot: pl.BlockSpec((None, tt, w), lambda bi, ti: (bi, ti, slot))
    bspec = lambda rows: pl.BlockSpec((None, rows, w), lambda bi, ti: (bi, 0, 0))
    lspec = lambda rows: pl.BlockSpec((None, rows, w), lambda bi, ti: (layer, 0, 0))
    in_specs = [zspec(s) for s in range(3, 10)]
    in_specs += [bspec(1), bspec(halos[0]), bspec(halos[1]), bspec(halos[2])]
    in_specs += [lspec(widths[0]), lspec(1),
                 pl.BlockSpec((None, 2, nb, blk, blk), lambda bi, ti: (layer, 0, 0, 0, 0)),
                 lspec(2), lspec(1), lspec(widths[1]), lspec(1), lspec(1), lspec(1), lspec(widths[2])]
    out_specs = [pl.BlockSpec((None, tt, 3 * w), lambda bi, ti: (bi, ti, 0)),
                 bspec(1), bspec(widths[0] - 1), bspec(widths[1] - 1), bspec(widths[2] - 1)]
    out_shape = [jax.ShapeDtypeStruct((b, t, 3 * w), BF16),
                 jax.ShapeDtypeStruct((b, 1, w), F32),
                 jax.ShapeDtypeStruct((b, widths[0] - 1, w), F32),
                 jax.ShapeDtypeStruct((b, widths[1] - 1, w), F32),
                 jax.ShapeDtypeStruct((b, widths[2] - 1, w), F32)]
    row3 = lambda a: a.reshape(a.shape[0], 1, a.shape[1])
    return pl.pallas_call(
        functools.partial(_mixers_kernel, tt=tt, n_t=n_t, halos=halos, widths=widths, blk=blk),
        grid=(b, n_t),
        in_specs=in_specs,
        out_specs=out_specs,
        out_shape=out_shape,
        scratch_shapes=[pltpu.VMEM((halos[0] + tt, w), F32), pltpu.VMEM((halos[1] + tt, w), F32),
                        pltpu.VMEM((halos[2] + tt, w), F32), pltpu.VMEM((1, w), F32),
                        pltpu.VMEM((SUBLANES, w), F32)],
        compiler_params=_cparams("parallel", "arbitrary"),
        name="mixers",
    )(*([z] * 7), h0, _pad_state(lru_st, halos[0]), _pad_state(cm_st, halos[1]), _pad_state(sc_st, halos[2]),
      P['lru_conv_w'], row3(P['lru_conv_b']), P['lru_gate_w'], P['lru_gate_b'], row3(P['lru_a_param']),
      P['cm_conv_w'], row3(P['cm_conv_b']), row3(P['cm_ln_w']), row3(P['cm_ln_b']), P['sc_conv_w'])


def _group_mixing(z, x_shape, pos_tables, states, P, layer, lam_init, dims, paged=None):
    b, t, _ = x_shape
    w, n_heads, head_dim = dims
    e = 2 * head_dim
    z = z.reshape(b, t, N_IN_SLOTS * w)
    cos, sin = pos_tables
    q, k_rows, v_rows, k16, v16 = _rope(z, cos, sin, w=w, n_heads=n_heads, head_dim=head_dim,
                                        q_dtype=BF16 if paged is None else F32)
    if paged is None:
        ya = _attn_prompt(q, k16, v16, P['attn_lambda'], P['attn_subln_w'], layer, lam_init,
                          n_heads=n_heads, head_dim=head_dim)
    else:
        cache_k, cache_v, page_table = paged
        ya = _attn_paged(q, k_rows.reshape(b, t, w), v_rows.reshape(b, t, w), cache_k, cache_v, page_table,
                         P['attn_lambda'], P['attn_subln_w'], layer, lam_init, n_heads=n_heads, head_dim=head_dim)
    ybcd, h_t, lru_buf, cm_buf, sc_buf = _mixers(z, states, P, layer, w=w)
    outs = (k_rows.reshape(b, t, n_heads, e), v_rows.reshape(b, t, n_heads, e), h_t.reshape(b, w),
            lru_buf, cm_buf, sc_buf)
    return ya.reshape(b * t, w), ybcd.reshape(b * t, 3 * w), outs


def _layer(xp, xs, tables, states, P, gu1, layer, lam_init, dims, paged, cast_next):
    nw = P['norm_w']
    d = xp.shape[-1]
    p2 = xp.reshape(-1, d)
    s2 = xs.reshape(-1, d)
    side = lambda *names: [(P[n], layer) for n in names]
    hp, hs, (wd1, w_in) = _norm_matmul(p2, s2, nw, gu1, layer, 0, swiglu=True, out_dtype=BF16,
                                       side=side('ffn1_w_down', 'w_in'))
    p2, s2, (gu2,) = _matmul_norm_res([hp], [hs], wd1, p2, s2, nw, layer, 1, 0.5, side=side('ffn2_w_gu'))
    zp, zs, (w_out,) = _norm_matmul(p2, s2, nw, w_in, layer, 2, swiglu=False, out_dtype=F32, tn_target=1024,
                                    side=side('w_out'))
    ya_p, ybcd_p, outs_p = _group_mixing(zp, xp.shape, tables[0], states[0], P, layer, lam_init, dims)
    ya_s, ybcd_s, outs_s = _group_mixing(zs, xs.shape, tables[1], states[1], P, layer, lam_init, dims, paged=paged)
    p2, s2, _ = _matmul_norm_res([ya_p, ybcd_p], [ya_s, ybcd_s], w_out, p2, s2, nw, layer, 3, 1.0)
    hp, hs, (wd2,) = _norm_matmul(p2, s2, nw, gu2, layer, 4, swiglu=True, out_dtype=BF16, side=side('ffn2_w_down'))
    next_side = [(P['ffn1_w_gu'], cast_next)] if cast_next is not None else []
    p2, s2, cast = _matmul_norm_res([hp], [hs], wd2, p2, s2, nw, layer, 5, 0.5, side=next_side)
    return p2.reshape(xp.shape), s2.reshape(xs.shape), outs_p, outs_s, (cast[0] if next_side else None)


def kernel(x_prompt, x_sample, cache_k, cache_v, state_lru_h, state_lru_conv, state_cm_conv, state_sc_conv, page_table, norm_w, ffn1_w_gu, ffn1_w_down, ffn2_w_gu, ffn2_w_down, w_in, w_out, attn_lambda, attn_subln_w, lru_conv_w, lru_conv_b, lru_gate_w, lru_gate_b, lru_a_param, cm_conv_w, cm_conv_b, cm_ln_w, cm_ln_b, sc_conv_w):
    b, t, d = x_prompt.shape
    db, ts, _ = x_sample.shape
    depth = norm_w.shape[0]
    w = d // N_GROUPS
    n_heads = cache_k.shape[3]
    e = cache_k.shape[4]
    head_dim = e // 2
    assert n_heads * e == w and w_in.shape[2] == N_IN_SLOTS * w
    n_pool, page = cache_k.shape[1], cache_k.shape[2]
    past = page_table.shape[1] * page
    dims = (w, n_heads, head_dim)

    P = {
        'norm_w': norm_w,
        'ffn1_w_gu': ffn1_w_gu, 'ffn1_w_down': ffn1_w_down, 'ffn2_w_gu': ffn2_w_gu,
        'ffn2_w_down': ffn2_w_down, 'w_in': w_in, 'w_out': w_out,
        'attn_lambda': attn_lambda,
        'attn_subln_w': attn_subln_w.reshape(depth, 1, e),
        'lru_conv_w': lru_conv_w, 'lru_conv_b': lru_conv_b, 'lru_gate_w': lru_gate_w,
        'lru_gate_b': lru_gate_b, 'lru_a_param': lru_a_param,
        'cm_conv_w': cm_conv_w, 'cm_conv_b': cm_conv_b, 'cm_ln_w': cm_ln_w, 'cm_ln_b': cm_ln_b,
        'sc_conv_w': sc_conv_w,
    }
    cache_k4 = cache_k.reshape(depth, n_pool, page * n_heads, e)
    cache_v4 = cache_v.reshape(depth, n_pool, page * n_heads, e)
    tables_p = _rope_tables(jnp.arange(t), w, head_dim)
    tables_s = _rope_tables(past + jnp.arange(ts), w, head_dim)
    zeros_p = (jnp.zeros((b, 1, w), F32), jnp.zeros((b, lru_conv_w.shape[1] - 1, w), F32),
               jnp.zeros((b, cm_conv_w.shape[1] - 1, w), F32), jnp.zeros((b, sc_conv_w.shape[1] - 1, w), F32))

    xp, xs = x_prompt, x_sample
    outs_p, outs_s = [], []
    gu1 = ffn1_w_gu[0].astype(BF16)
    for l in range(depth):
        lam_init = 0.8 - 0.6 * math.exp(-0.3 * l)
        st_s = (state_lru_h[l].reshape(db, 1, w), state_lru_conv[l], state_cm_conv[l], state_sc_conv[l])
        xp, xs, sp, ss, gu1 = _layer(xp, xs, (tables_p, tables_s), (zeros_p, st_s), P, gu1, l, lam_init, dims,
                                     (cache_k4, cache_v4, page_table), l + 1 if l + 1 < depth else None)
        outs_p.append(sp)
        outs_s.append(ss)
    stack = lambda outs, i: jnp.stack([o[i] for o in outs])
    return (xp, xs) + tuple(stack(outs_p, i) for i in range(6)) + tuple(stack(outs_s, i) for i in range(6))
```

```python
import functools
import math

import jax
import jax.numpy as jnp
from jax import lax
from jax.experimental import pallas as pl
from jax.experimental.pallas import tpu as pltpu

F32 = jnp.float32
BF16 = jnp.bfloat16

NORM_EPS = 1e-6
ROPE_THETA = 10000.0
LRU_C = 8.0
N_GROUPS = 4
N_IN_SLOTS = 10
VMEM_LIMIT_BYTES = 56 * 1024 * 1024
LANES = 128
SUBLANES = 8
BF16_ROWS = 16
ROW_TILE = 1024
COL_TILE = 512
COL_TILE_WIDE = 1024
ROW_TILE_RESIDENT = 512
ROW_TILE_RESIDENT_LONG_K = 256
LONG_K = 4096
SUB_TILE_ROWS = 512
ATTN_TILE = 256
ROPE_ROWS = 512
MIXER_ROWS = 256
CONV_ROW_BLOCK = 64
PAGES_PER_STEP = 16


def _cparams(*sem):
    return pltpu.CompilerParams(dimension_semantics=sem, vmem_limit_bytes=VMEM_LIMIT_BYTES)


def _pick_tile(n, target, quantum):
    if n <= target:
        return n
    t = (target // quantum) * quantum
    while t >= quantum:
        if n % t == 0:
            return t
        t -= quantum
    return n


def _dot(a, b):
    return jnp.dot(a, b, preferred_element_type=F32)


def _dot_nt(a, b):
    return lax.dot_general(a, b, (((1,), (1,)), ((), ())), preferred_element_type=F32)


def _rms_rows(x, w_row):
    ms = jnp.mean(x * x, axis=-1, keepdims=True)
    return x * lax.rsqrt(ms + NORM_EPS) * w_row


def _side_cast_plan(srcs, n_steps, step_index):
    in_specs, out_specs, out_shapes = [], [], []
    for src, layer in srcs:
        _, r, c = src.shape
        nb = next(n for n in range(min(n_steps, r // BF16_ROWS), 0, -1) if r % n == 0 and (r // n) % BF16_ROWS == 0)
        rb = r // nb

        def in_map(*ids, layer=layer, nb=nb):
            return (layer, jnp.minimum(step_index(*ids), nb - 1), 0)

        def out_map(*ids, nb=nb):
            return (jnp.minimum(step_index(*ids), nb - 1), 0)

        in_specs.append(pl.BlockSpec((None, rb, c), in_map))
        out_specs.append(pl.BlockSpec((rb, c), out_map))
        out_shapes.append(jax.ShapeDtypeStruct((r, c), BF16))
    return in_specs, out_specs, out_shapes


def _run_side_casts(src_refs, dst_refs):
    for src_ref, dst_ref in zip(src_refs, dst_refs):
        dst_ref[...] = src_ref[...].astype(BF16)


def _norm_matmul_kernel(x_ref, xr_ref, nw_ref, *refs, norm_row, swiglu, n_side, tm):
    n_w = 2 if swiglu else 1
    w_refs = refs[:n_w]
    side_src = refs[n_w:n_w + n_side]
    out_ref, outr_ref = refs[n_w + n_side:n_w + n_side + 2]
    side_dst = refs[n_w + n_side + 2:n_w + 2 * n_side + 2]
    xn_ref = refs[-1]
    i = pl.program_id(0)
    j = pl.program_id(1)
    rows_all = xn_ref.shape[0]
    nw_row = nw_ref[norm_row:norm_row + 1, :]

    @pl.when(j == 0)
    def _():
        xn_ref[0:tm, :] = _rms_rows(x_ref[...], nw_row).astype(BF16)

    @pl.when((j == 0) & (i == 0))
    def _():
        xn_ref[tm:rows_all, :] = _rms_rows(xr_ref[...], nw_row).astype(BF16)

    def product(rows):
        xn = xn_ref[0:rows, :]
        if swiglu:
            g = _dot(xn, w_refs[0][...])
            u = _dot(xn, w_refs[1][...])
            return g * jax.nn.sigmoid(g) * u
        return _dot(xn, w_refs[0][...])

    @pl.when(i == 0)
    def _():
        r = product(rows_all)
        out_ref[...] = r[0:tm, :].astype(out_ref.dtype)
        outr_ref[...] = r[tm:rows_all, :].astype(outr_ref.dtype)
        _run_side_casts(side_src, side_dst)

    @pl.when(i != 0)
    def _():
        out_ref[...] = product(tm).astype(out_ref.dtype)
        _run_side_casts(side_src, side_dst)


def _norm_matmul(x, x_rider, norm_w, w, layer, norm_row, *, swiglu, out_dtype, tn_target=COL_TILE, side=()):
    m, d = x.shape
    mr = x_rider.shape[0]
    n = w.shape[1]
    n_out = n // 2 if swiglu else n
    tm = _pick_tile(m, ROW_TILE, BF16_ROWS)
    tn = _pick_tile(n_out, tn_target, LANES)
    n_blocks = n_out // tn
    grid = (m // tm, n_blocks)
    in_specs = [
        pl.BlockSpec((tm, d), lambda i, j: (i, 0)),
        pl.BlockSpec((mr, d), lambda i, j: (0, 0)),
        pl.BlockSpec((None, norm_w.shape[1], d), lambda i, j: (layer, 0, 0)),
        pl.BlockSpec((d, tn), lambda i, j: (0, j)),
    ]
    args = [x, x_rider, norm_w, w]
    if swiglu:
        in_specs.append(pl.BlockSpec((d, tn), lambda i, j: (0, j + n_blocks)))
        args.append(w)
    s_in, s_out, s_shapes = _side_cast_plan(side, grid[0] * grid[1], lambda i, j: i * n_blocks + j)
    rider_spec = pl.BlockSpec((mr, tn), lambda i, j: (0, jnp.where(i == 0, j, n_blocks - 1)))
    outs = pl.pallas_call(
        functools.partial(_norm_matmul_kernel, norm_row=norm_row, swiglu=swiglu, n_side=len(side), tm=tm),
        grid=grid,
        in_specs=in_specs + s_in,
        out_specs=[pl.BlockSpec((tm, tn), lambda i, j: (i, j)), rider_spec] + s_out,
        out_shape=[jax.ShapeDtypeStruct((m, n_out), out_dtype), jax.ShapeDtypeStruct((mr, n_out), out_dtype)]
                  + s_shapes,
        scratch_shapes=[pltpu.VMEM((tm + mr, d), BF16)],
        compiler_params=_cparams("arbitrary", "arbitrary"),
        name="norm_matmul_swiglu" if swiglu else "norm_matmul",
    )(*args, *[src for src, _ in side])
    return outs[0], outs[1], outs[2:]


def _matmul_norm_res_kernel(*refs, n_parts, part_rows, norm_row, scale, n_side, tm, n_sub):
    h_refs = refs[:n_parts]
    hr_refs = refs[n_parts:2 * n_parts]
    w_ref, x_ref, xr_ref, nw_ref = refs[2 * n_parts:2 * n_parts + 4]
    base = 2 * n_parts + 4
    side_src = refs[base:base + n_side]
    out_ref, outr_ref = refs[base + n_side:base + n_side + 2]
    side_dst = refs[base + n_side + 2:base + 2 * n_side + 2]
    lhs_refs = refs[base + 2 * n_side + 2:]
    i = pl.program_id(0)
    nw_row = nw_ref[norm_row:norm_row + 1, :]

    def product(lhs_list):
        y = None
        row = 0
        for lhs, rows in zip(lhs_list, part_rows):
            part = _dot(lhs, w_ref[row:row + rows, :])
            y = part if y is None else y + part
            row += rows
        return _rms_rows(y, nw_row)

    @pl.when(i == 0)
    def _():
        for h_ref, hr_ref, lhs_ref in zip(h_refs, hr_refs, lhs_refs):
            lhs_ref[0:tm, :] = h_ref[...]
            lhs_ref[tm:lhs_ref.shape[0], :] = hr_ref[...]
        yn = product([lhs_ref[...] for lhs_ref in lhs_refs])
        out_ref[...] = x_ref[...] + scale * yn[0:tm, :]
        outr_ref[...] = xr_ref[...] + scale * yn[tm:yn.shape[0], :]
        _run_side_casts(side_src, side_dst)

    @pl.when(i != 0)
    def _():
        sub = tm // n_sub
        for r0 in range(0, tm, sub):
            out_ref[r0:r0 + sub, :] = (x_ref[r0:r0 + sub, :]
                                       + scale * product([h_ref[r0:r0 + sub, :] for h_ref in h_refs]))
        _run_side_casts(side_src, side_dst)


def _matmul_norm_res(h_parts, h_rider_parts, w, x, x_rider, norm_w, layer, norm_row, scale, side=()):
    m, d = x.shape
    mr = x_rider.shape[0]
    k = w.shape[0]
    part_rows = tuple(h.shape[1] for h in h_parts)
    assert sum(part_rows) == k
    tm = _pick_tile(m, ROW_TILE_RESIDENT_LONG_K if k > LONG_K else ROW_TILE_RESIDENT, BF16_ROWS)
    in_specs = [pl.BlockSpec((tm, r), lambda i: (i, 0)) for r in part_rows]
    in_specs += [pl.BlockSpec((mr, r), lambda i: (0, 0)) for r in part_rows]
    in_specs += [
        pl.BlockSpec((k, d), lambda i: (0, 0), pipeline_mode=pl.Buffered(1)),
        pl.BlockSpec((tm, d), lambda i: (i, 0)),
        pl.BlockSpec((mr, d), lambda i: (0, 0)),
        pl.BlockSpec((None, norm_w.shape[1], d), lambda i: (layer, 0, 0)),
    ]
    s_in, s_out, s_shapes = _side_cast_plan(side, m // tm, lambda i: i)
    outs = pl.pallas_call(
        functools.partial(_matmul_norm_res_kernel, n_parts=len(h_parts), part_rows=part_rows,
                          norm_row=norm_row, scale=scale, n_side=len(side), tm=tm,
                          n_sub=2 if tm % SUB_TILE_ROWS == 0 else 1),
        grid=(m // tm,),
        in_specs=in_specs + s_in,
        out_specs=[pl.BlockSpec((tm, d), lambda i: (i, 0)), pl.BlockSpec((mr, d), lambda i: (0, 0))] + s_out,
        out_shape=[jax.ShapeDtypeStruct((m, d), F32), jax.ShapeDtypeStruct((mr, d), F32)] + s_shapes,
        scratch_shapes=[pltpu.VMEM((tm + mr, r), BF16) for r in part_rows],
        compiler_params=_cparams("arbitrary"),
        name="matmul_norm_res",
    )(*h_parts, *h_rider_parts, w, x, x_rider, norm_w, *[src for src, _ in side])
    return outs[0], outs[1], outs[2:]


def _swap_halves(x, half):
    n = x.shape[-1]
    lane = lax.broadcasted_iota(jnp.int32, x.shape, x.ndim - 1)
    first = (lane & half) == 0
    return jnp.where(first, pltpu.roll(x, n - half, x.ndim - 1), pltpu.roll(x, half, x.ndim - 1))


def _rope_kernel(q_ref, k_ref, v_ref, cos_ref, sin_ref, qo_ref, ko_ref, vo_ref, kb_ref, vb_ref,
                 *, half, q_scale, n_heads):
    cos = cos_ref[...]
    sin = sin_ref[...]
    q = q_ref[...]
    k = k_ref[...]
    v = v_ref[...]
    qr = q * cos + _swap_halves(q, half) * sin
    kr = k * cos + _swap_halves(k, half) * sin
    qo_ref[...] = (qr * q_scale).astype(qo_ref.dtype)
    kb_ref[...] = kr.astype(BF16)
    vb_ref[...] = v.astype(BF16)
    tt = k.shape[0]
    e = k.shape[1] // n_heads
    for h in range(n_heads):
        ko_ref[pl.ds(h, tt, stride=n_heads), :] = kr[:, h * e:(h + 1) * e]
        vo_ref[pl.ds(h, tt, stride=n_heads), :] = v[:, h * e:(h + 1) * e]


def _rope(z, cos, sin, *, w, n_heads, head_dim, q_dtype):
    b, t, _ = z.shape
    e = w // n_heads
    tt = _pick_tile(t, ROPE_ROWS, BF16_ROWS)
    zspec = lambda slot: pl.BlockSpec((None, tt, w), lambda bi, ti: (bi, ti, slot))
    tspec = pl.BlockSpec((tt, w), lambda bi, ti: (ti, 0))
    ospec = pl.BlockSpec((None, tt, w), lambda bi, ti: (bi, ti, 0))
    rspec = pl.BlockSpec((None, tt * n_heads, e), lambda bi, ti: (bi, ti, 0))
    rows_shape = jax.ShapeDtypeStruct((b, t * n_heads, e), F32)
    return pl.pallas_call(
        functools.partial(_rope_kernel, half=head_dim // 2, q_scale=head_dim ** -0.5, n_heads=n_heads),
        grid=(b, t // tt),
        in_specs=[zspec(0), zspec(1), zspec(2), tspec, tspec],
        out_specs=[ospec, rspec, rspec, ospec, ospec],
        out_shape=[jax.ShapeDtypeStruct((b, t, w), q_dtype), rows_shape, rows_shape,
                   jax.ShapeDtypeStruct((b, t, w), BF16), jax.ShapeDtypeStruct((b, t, w), BF16)],
        compiler_params=_cparams("parallel", "parallel"),
        name="rope",
    )(z, z, z, cos, sin)


def _rope_tables(pos, w, head_dim):
    half = head_dim // 2
    inv = ROPE_THETA ** (-jnp.arange(half, dtype=F32) / half)
    ang = pos.astype(F32)[:, None] * inv[None, :]
    cos = jnp.cos(ang)
    sin = jnp.sin(ang)
    cos_d = jnp.concatenate([cos, cos], axis=-1)
    sin_d = jnp.concatenate([-sin, sin], axis=-1)
    reps = w // head_dim
    return jnp.tile(cos_d, (1, reps)), jnp.tile(sin_d, (1, reps))


def _lambda_value(lp, lam_init):
    s1 = jnp.sum(lp[0:1, :] * lp[1:2, :], axis=-1, keepdims=True)
    s2 = jnp.sum(lp[2:3, :] * lp[3:4, :], axis=-1, keepdims=True)
    return jnp.exp(s1) - jnp.exp(s2) + lam_init


def _attn_prompt_kernel(q_ref, k_ref, v_ref, lp_ref, sw_ref, o_ref, *, tile, head_dim, lam_init):
    t, e = q_ref.shape
    lam = _lambda_value(lp_ref[...], lam_init)
    lane = lax.broadcasted_iota(jnp.int32, (tile, e), 1)
    row = lax.broadcasted_iota(jnp.int32, (tile, tile), 0)
    col = lax.broadcasted_iota(jnp.int32, (tile, tile), 1)
    keep = col <= row
    keep2 = jnp.concatenate([keep, keep], axis=0)
    for i in range(t // tile):
        kv = (i + 1) * tile
        q = q_ref[i * tile:kv, :]
        ks = k_ref[0:kv, :]
        vx = jnp.concatenate([v_ref[0:kv, :], jnp.ones((kv, e), BF16)], axis=1)
        qc = jnp.concatenate([jnp.where(lane < head_dim, q, jnp.zeros_like(q)),
                              jnp.where(lane >= head_dim, q, jnp.zeros_like(q))], axis=0)
        s = _dot_nt(qc, ks)
        s_diag = jnp.where(keep2, s[:, i * tile:], -jnp.inf)
        m = jnp.max(s_diag, axis=-1, keepdims=True)
        if i > 0:
            s_full = s[:, :i * tile]
            m = jnp.maximum(m, jnp.max(s_full, axis=-1, keepdims=True))
            p = jnp.concatenate([jnp.exp(s_full - m), jnp.exp(s_diag - m)], axis=1)
        else:
            p = jnp.exp(s_diag - m)
        acc = _dot(p.astype(BF16), vx)
        d = acc[:, :e] / acc[:, e:]
        normed = [d[:tile], d[tile:]]
        o = normed[0] - lam * normed[1]
        o_ref[i * tile:kv, :] = (_rms_rows(o, sw_ref[...]) * (1.0 - lam_init)).astype(o_ref.dtype)


def _attn_prompt(q, k, v, attn_lambda, subln_w, layer, lam_init, *, n_heads, head_dim):
    b, t, w = q.shape
    e = 2 * head_dim
    tile = _pick_tile(t, ATTN_TILE, LANES)
    head_spec = pl.BlockSpec((None, t, e), lambda bi, h: (bi, 0, h))
    return pl.pallas_call(
        functools.partial(_attn_prompt_kernel, tile=tile, head_dim=head_dim, lam_init=lam_init),
        grid=(b, n_heads),
        in_specs=[
            head_spec, head_spec, head_spec,
            pl.BlockSpec((None, 4, head_dim), lambda bi, h: (layer, 0, 0)),
            pl.BlockSpec((None, 1, e), lambda bi, h: (layer, 0, 0)),
        ],
        out_specs=head_spec,
        out_shape=jax.ShapeDtypeStruct((b, t, w), BF16),
        compiler_params=_cparams("parallel", "parallel"),
        name="attn_prompt",
    )(q, k, v, attn_lambda, subln_w)


def _attn_paged_kernel(pt_ref, q_ref, kn_ref, vn_ref, lp_ref, sw_ref, ck_ref, cv_ref, o_ref,
                       kbuf, vbuf, sem, qb_ref, m_ref, l_ref, acc_ref,
                       *, layer, pps, n_heads, head_dim, ts, lam_init):
    step = pl.program_id(1)
    n_steps = pl.num_programs(1)
    flat = pl.program_id(0) * n_steps + step
    total = pl.num_programs(0) * n_steps
    slot = flat % 2
    e = 2 * head_dim
    half = n_heads * ts
    rows = 2 * half
    page_rows = kbuf.shape[2]

    def page_copies(b_idx, s_idx, slot_idx):
        out = []
        for r in range(pps):
            page_id = pt_ref[b_idx, s_idx * pps + r]
            out.append(pltpu.make_async_copy(ck_ref.at[layer, page_id], kbuf.at[slot_idx, r], sem.at[slot_idx, 0]))
            out.append(pltpu.make_async_copy(cv_ref.at[layer, page_id], vbuf.at[slot_idx, r], sem.at[slot_idx, 1]))
        return out

    @pl.when(flat == 0)
    def _():
        for c in page_copies(0, 0, 0):
            c.start()

    nxt = jnp.where(flat + 1 < total, flat + 1, 0)
    for c in page_copies(nxt // n_steps, nxt % n_steps, 1 - slot):
        c.start()
    for c in page_copies(pl.program_id(0), step, slot):
        c.wait()

    row_i = lax.broadcasted_iota(jnp.int32, (rows, e), 0)
    lane_i = lax.broadcasted_iota(jnp.int32, (rows, e), 1)
    row_c = row_i // half
    row_h = (row_i % half) // ts
    row_t = row_i % ts

    @pl.when(step == 0)
    def _():
        own_comp = (lane_i // head_dim) == row_c
        qb = jnp.zeros((rows, e), F32)
        for h in range(n_heads):
            for t in range(ts):
                qb = jnp.where(own_comp & (row_h == h) & (row_t == t), q_ref[t:t + 1, h * e:(h + 1) * e], qb)
        qb_ref[...] = qb
        m_ref[...] = jnp.full(m_ref.shape, -jnp.inf, F32)
        l_ref[...] = jnp.zeros(l_ref.shape, F32)
        acc_ref[...] = jnp.zeros(acc_ref.shape, F32)

    qb = qb_ref[...]
    qb16 = qb.astype(BF16)
    col_h = lax.broadcasted_iota(jnp.int32, (rows, page_rows), 1) % n_heads
    own_cols = col_h == (lax.broadcasted_iota(jnp.int32, (rows, page_rows), 0) % half) // ts
    s = [jnp.where(own_cols, _dot_nt(qb16, kbuf[slot, r].astype(BF16)), -jnp.inf) for r in range(pps)]
    m_old = m_ref[...]
    m_new = m_old
    for r in range(pps):
        m_new = jnp.maximum(m_new, jnp.max(s[r], axis=-1, keepdims=True))
    alpha = jnp.exp(m_old - m_new)
    l_new = alpha * l_ref[...]
    acc = alpha * acc_ref[...]
    for r in range(pps):
        p = jnp.exp(s[r] - m_new)
        l_new = l_new + jnp.sum(p, axis=-1, keepdims=True)
        acc = acc + _dot(p.astype(BF16), vbuf[slot, r].astype(BF16))
    m_ref[...] = m_new
    l_ref[...] = l_new
    acc_ref[...] = acc

    @pl.when(flat == total - 1)
    def _():
        for c in page_copies(0, 0, 1 - slot):
            c.wait()

    @pl.when(step == n_steps - 1)
    def _():
        m = m_ref[...]
        l = l_ref[...]
        a = acc_ref[...]

        def own_head_rows(ref, t):
            out = jnp.zeros((rows, e), F32)
            for h in range(n_heads):
                out = jnp.where(row_h == h, ref[t:t + 1, h * e:(h + 1) * e], out)
            return out

        row_t1 = row_t[:, 0:1]
        s_new = []
        for t in range(ts):
            s_t = jnp.sum(qb * own_head_rows(kn_ref, t), axis=-1, keepdims=True)
            s_new.append(jnp.where(row_t1 >= t, s_t, -jnp.inf))
        m_fin = m
        for t in range(ts):
            m_fin = jnp.maximum(m_fin, s_new[t])
        alpha_f = jnp.exp(m - m_fin)
        l = alpha_f * l
        a = alpha_f * a
        for t in range(ts):
            p_t = jnp.exp(s_new[t] - m_fin)
            l = l + p_t
            a = a + p_t * own_head_rows(vn_ref, t)
        d = a / l
        lam = _lambda_value(lp_ref[...], lam_init)
        o = d[0:half, :] - lam * d[half:rows, :]
        on = _rms_rows(o, sw_ref[...]) * (1.0 - lam_init)
        for h in range(n_heads):
            o_ref[:, h * e:(h + 1) * e] = on[h * ts:(h + 1) * ts, :].astype(o_ref.dtype)


def _attn_paged(q, k_new, v_new, cache_k, cache_v, page_table, attn_lambda, subln_w, layer, lam_init,
                *, n_heads, head_dim):
    db, ts, w = q.shape
    e = 2 * head_dim
    page_rows = cache_k.shape[2]
    n_pages = page_table.shape[1]
    pps = _pick_tile(n_pages, PAGES_PER_STEP, 1)
    rows = 2 * n_heads * ts
    tok_spec = pl.BlockSpec((None, ts, w), lambda b, s, pt: (b, 0, 0))
    grid_spec = pltpu.PrefetchScalarGridSpec(
        num_scalar_prefetch=1,
        grid=(db, n_pages // pps),
        in_specs=[tok_spec, tok_spec, tok_spec,
                  pl.BlockSpec((None, 4, head_dim), lambda b, s, pt: (layer, 0, 0)),
                  pl.BlockSpec((None, 1, e), lambda b, s, pt: (layer, 0, 0)),
                  pl.BlockSpec(memory_space=pl.ANY), pl.BlockSpec(memory_space=pl.ANY)],
        out_specs=pl.BlockSpec((None, ts, w), lambda b, s, pt: (b, 0, 0)),
        scratch_shapes=[pltpu.VMEM((2, pps, page_rows, e), F32), pltpu.VMEM((2, pps, page_rows, e), F32),
                        pltpu.SemaphoreType.DMA((2, 2)),
                        pltpu.VMEM((rows, e), F32), pltpu.VMEM((rows, 1), F32),
                        pltpu.VMEM((rows, 1), F32), pltpu.VMEM((rows, e), F32)],
    )
    return pl.pallas_call(
        functools.partial(_attn_paged_kernel, layer=layer, pps=pps, n_heads=n_heads, head_dim=head_dim, ts=ts,
                          lam_init=lam_init),
        grid_spec=grid_spec,
        out_shape=jax.ShapeDtypeStruct((db, ts, w), BF16),
        compiler_params=_cparams("arbitrary", "arbitrary"),
        name="attn_paged",
    )(page_table, q, k_new, v_new, attn_lambda, subln_w, cache_k, cache_v)


def _gelu_tanh(x):
    return 0.5 * x * (1.0 + jnp.tanh(math.sqrt(2.0 / math.pi) * (x + 0.044715 * (x * x * x))))


def _conv_from_buffer(buf_ref, w_ref, *, halo, width, rows, row_block):
    first = halo - (width - 1)
    outs = []
    for r0 in range(0, rows, row_block):
        rb = min(row_block, rows - r0)
        acc = None
        for phase in range(min(SUBLANES, width)):
            taps = list(range(phase, width, SUBLANES))
            span = rb + (taps[-1] - phase)
            zs = buf_ref[pl.ds(first + r0 + phase, span), :]
            part = None
            for j in taps:
                term = w_ref[j:j + 1, :] * zs[j - phase:j - phase + rb, :]
                part = term if part is None else part + term
            acc = part if acc is None else acc + part
        outs.append(acc)
    return outs[0] if len(outs) == 1 else jnp.concatenate(outs, axis=0)


def _mixers_kernel(xb_ref, gb_ref, ca_ref, cb_ref, sb_ref, sc_ref, sx_ref,
                   h0_ref, lru_st_ref, cm_st_ref, sc_st_ref,
                   lcw_ref, lcb_ref, gw_ref, gbias_ref, ap_ref,
                   ccw_ref, ccb_ref, lnw_ref, lnb_ref, scw_ref,
                   y_ref, h_out_ref, lru_out_ref, cm_out_ref, sc_out_ref,
                   lru_buf, cm_buf, sc_buf, h_ref, hs_buf,
                   *, tt, n_t, halos, widths, blk):
    ti = pl.program_id(1)
    w = xb_ref.shape[-1]
    (halo_l, halo_c, halo_s), (wid_l, wid_c, wid_s) = halos, widths

    @pl.when(ti == 0)
    def _():
        lru_buf[0:halo_l, :] = lru_st_ref[...]
        cm_buf[0:halo_c, :] = cm_st_ref[...]
        sc_buf[0:halo_s, :] = sc_st_ref[...]
        h_ref[...] = h0_ref[...]

    lru_buf[halo_l:halo_l + tt, :] = xb_ref[...]
    xc = _conv_from_buffer(lru_buf, lcw_ref, halo=halo_l, width=wid_l, rows=tt, row_block=tt) + lcb_ref[...]
    xc16 = xc.astype(BF16)
    gates = []
    for g in range(2):
        parts = [_dot(xc16[:, n * blk:(n + 1) * blk], gw_ref[g, n].astype(BF16)) for n in range(w // blk)]
        gates.append(jnp.concatenate(parts, axis=1) + gbias_ref[g:g + 1, :])
    r_gate = jax.nn.sigmoid(gates[0])
    i_gate = jax.nn.sigmoid(gates[1])
    log_a = -LRU_C * r_gate * jax.nn.softplus(-ap_ref[...])
    a = jnp.exp(log_a)
    u = jnp.sqrt(1.0 - jnp.exp(2.0 * log_a)) * (i_gate * xc)
    if tt % SUBLANES == 0:
        groups = tt // SUBLANES
        a3 = a.reshape(groups, SUBLANES, w)
        u3 = u.reshape(groups, SUBLANES, w)
        sub = lax.broadcasted_iota(jnp.int32, (groups, SUBLANES, w), 1)
        dist = 1
        while dist < SUBLANES:
            live = sub >= dist
            a_prev = jnp.where(live, pltpu.roll(a3, dist, 1), 1.0)
            u_prev = jnp.where(live, pltpu.roll(u3, dist, 1), 0.0)
            u3 = a3 * u_prev + u3
            a3 = a3 * a_prev
            dist *= 2
        h_last = h_ref[...]
        blocks = []
        for g in range(groups):
            blk_h = u3[g] + a3[g] * h_last
            blocks.append(blk_h)
            h_last = blk_h[SUBLANES - 1:SUBLANES, :]
        hs = jnp.concatenate(blocks, axis=0)
    else:
        h_last = h_ref[...]
        for r in range(tt):
            h_last = a[r:r + 1, :] * h_last + u[r:r + 1, :]
            hs_buf[r:r + 1, :] = h_last
        hs = hs_buf[0:tt, :]
    h_ref[...] = h_last
    y_ref[:, 0:w] = (_gelu_tanh(gb_ref[...]) * hs).astype(y_ref.dtype)

    cm_buf[halo_c:halo_c + tt, :] = ca_ref[...] * jax.nn.sigmoid(cb_ref[...])
    uc = _conv_from_buffer(cm_buf, ccw_ref, halo=halo_c, width=wid_c, rows=tt, row_block=CONV_ROW_BLOCK) + ccb_ref[...]
    mu = jnp.mean(uc, axis=-1, keepdims=True)
    cen = uc - mu
    var = jnp.mean(cen * cen, axis=-1, keepdims=True)
    ln = cen * lax.rsqrt(var + NORM_EPS) * lnw_ref[...] + lnb_ref[...]
    y_ref[:, w:2 * w] = (ln * jax.nn.sigmoid(ln)).astype(y_ref.dtype)

    sc_buf[halo_s:halo_s + tt, :] = sc_ref[...] * sx_ref[...]
    sconv = _conv_from_buffer(sc_buf, scw_ref, halo=halo_s, width=wid_s, rows=tt, row_block=tt)
    y_ref[:, 2 * w:3 * w] = (sb_ref[...] * sconv).astype(y_ref.dtype)

    @pl.when(ti == n_t - 1)
    def _():
        h_out_ref[...] = h_last
        lru_out_ref[...] = lru_buf[pl.ds(tt + halo_l - (wid_l - 1), wid_l - 1), :]
        cm_out_ref[...] = cm_buf[pl.ds(tt + halo_c - (wid_c - 1), wid_c - 1), :]
        sc_out_ref[...] = sc_buf[pl.ds(tt + halo_s - (wid_s - 1), wid_s - 1), :]

    if n_t > 1:
        @pl.when(ti < n_t - 1)
        def _():
            lru_buf[0:halo_l, :] = lru_buf[tt:tt + halo_l, :]
            cm_buf[0:halo_c, :] = cm_buf[tt:tt + halo_c, :]
            sc_buf[0:halo_s, :] = sc_buf[tt:tt + halo_s, :]


def _pad_state(st, halo):
    return jnp.pad(st, ((0, 0), (halo - st.shape[1], 0), (0, 0)))


def _mixers(z, states, P, layer, *, w):
    b, t, _ = z.shape
    h0, lru_st, cm_st, sc_st = states
    widths = (P['lru_conv_w'].shape[1], P['cm_conv_w'].shape[1], P['sc_conv_w'].shape[1])
    halos = tuple(-(-(wd - 1) // SUBLANES) * SUBLANES for wd in widths)
    tt = _pick_tile(t, MIXER_ROWS, SUBLANES)
    n_t = t // tt
    nb = P['lru_gate_w'].shape[2]
    blk = w // nb
    zspec = lambda slot: pl.BlockSpec((None, tt, w), lambda bi, ti: (bi, ti, slot))
    bspec = lambda rows: pl.BlockSpec((None, rows, w), lambda bi, ti: (bi, 0, 0))
    lspec = lambda rows: pl.BlockSpec((None, rows, w), lambda bi, ti: (layer, 0, 0))
    in_specs = [zspec(s) for s in range(3, 10)]
    in_specs += [bspec(1), bspec(halos[0]), bspec(halos[1]), bspec(halos[2])]
    in_specs += [lspec(widths[0]), lspec(1),
                 pl.BlockSpec((None, 2, nb, blk, blk), lambda bi, ti: (layer, 0, 0, 0, 0)),
                 lspec(2), lspec(1), lspec(widths[1]), lspec(1), lspec(1), lspec(1), lspec(widths[2])]
    out_specs = [pl.BlockSpec((None, tt, 3 * w), lambda bi, ti: (bi, ti, 0)),
                 bspec(1), bspec(widths[0] - 1), bspec(widths[1] - 1), bspec(widths[2] - 1)]
    out_shape = [jax.ShapeDtypeStruct((b, t, 3 * w), BF16),
                 jax.ShapeDtypeStruct((b, 1, w), F32),
                 jax.ShapeDtypeStruct((b, widths[0] - 1, w), F32),
                 jax.ShapeDtypeStruct((b, widths[1] - 1, w), F32),
                 jax.ShapeDtypeStruct((b, widths[2] - 1, w), F32)]
    row3 = lambda a: a.reshape(a.shape[0], 1, a.shape[1])
    return pl.pallas_call(
        functools.partial(_mixers_kernel, tt=tt, n_t=n_t, halos=halos, widths=widths, blk=blk),
        grid=(b, n_t),
        in_specs=in_specs,
        out_specs=out_specs,
        out_shape=out_shape,
        scratch_shapes=[pltpu.VMEM((halos[0] + tt, w), F32), pltpu.VMEM((halos[1] + tt, w), F32),
                        pltpu.VMEM((halos[2] + tt, w), F32), pltpu.VMEM((1, w), F32),
                        pltpu.VMEM((SUBLANES, w), F32)],
        compiler_params=_cparams("parallel", "arbitrary"),
        name="mixers",
    )(*([z] * 7), h0, _pad_state(lru_st, halos[0]), _pad_state(cm_st, halos[1]), _pad_state(sc_st, halos[2]),
      P['lru_conv_w'], row3(P['lru_conv_b']), P['lru_gate_w'], P['lru_gate_b'], row3(P['lru_a_param']),
      P['cm_conv_w'], row3(P['cm_conv_b']), row3(P['cm_ln_w']), row3(P['cm_ln_b']), P['sc_conv_w'])


def _group_mixing(z, x_shape, pos_tables, states, P, layer, lam_init, dims, paged=None):
    b, t, _ = x_shape
    w, n_heads, head_dim = dims
    e = 2 * head_dim
    z = z.reshape(b, t, N_IN_SLOTS * w)
    cos, sin = pos_tables
    q, k_rows, v_rows, k16, v16 = _rope(z, cos, sin, w=w, n_heads=n_heads, head_dim=head_dim,
                                        q_dtype=BF16 if paged is None else F32)
    if paged is None:
        ya = _attn_prompt(q, k16, v16, P['attn_lambda'], P['attn_subln_w'], layer, lam_init,
                          n_heads=n_heads, head_dim=head_dim)
    else:
        cache_k, cache_v, page_table = paged
        ya = _attn_paged(q, k_rows.reshape(b, t, w), v_rows.reshape(b, t, w), cache_k, cache_v, page_table,
                         P['attn_lambda'], P['attn_subln_w'], layer, lam_init, n_heads=n_heads, head_dim=head_dim)
    ybcd, h_t, lru_buf, cm_buf, sc_buf = _mixers(z, states, P, layer, w=w)
    outs = (k_rows.reshape(b, t, n_heads, e), v_rows.reshape(b, t, n_heads, e), h_t.reshape(b, w),
            lru_buf, cm_buf, sc_buf)
    return ya.reshape(b * t, w), ybcd.reshape(b * t, 3 * w), outs


def _layer(xp, xs, tables, states, P, gu1, layer, lam_init, dims, paged, cast_next):
    nw = P['norm_w']
    d = xp.shape[-1]
    p2 = xp.reshape(-1, d)
    s2 = xs.reshape(-1, d)
    side = lambda *names: [(P[n], layer) for n in names]
    hp, hs, (wd1, w_in) = _norm_matmul(p2, s2, nw, gu1, layer, 0, swiglu=True, out_dtype=BF16,
                                       side=side('ffn1_w_down', 'w_in'))
    p2, s2, (gu2,) = _matmul_norm_res([hp], [hs], wd1, p2, s2, nw, layer, 1, 0.5, side=side('ffn2_w_gu'))
    zp, zs, (w_out,) = _norm_matmul(p2, s2, nw, w_in, layer, 2, swiglu=False, out_dtype=F32, tn_target=COL_TILE_WIDE,
                                    side=side('w_out'))
    ya_p, ybcd_p, outs_p = _group_mixing(zp, xp.shape, tables[0], states[0], P, layer, lam_init, dims)
    ya_s, ybcd_s, outs_s = _group_mixing(zs, xs.shape, tables[1], states[1], P, layer, lam_init, dims, paged=paged)
    p2, s2, _ = _matmul_norm_res([ya_p, ybcd_p], [ya_s, ybcd_s], w_out, p2, s2, nw, layer, 3, 1.0)
    hp, hs, (wd2,) = _norm_matmul(p2, s2, nw, gu2, layer, 4, swiglu=True, out_dtype=BF16, side=side('ffn2_w_down'))
    next_side = [(P['ffn1_w_gu'], cast_next)] if cast_next is not None else []
    p2, s2, cast = _matmul_norm_res([hp], [hs], wd2, p2, s2, nw, layer, 5, 0.5, side=next_side)
    return p2.reshape(xp.shape), s2.reshape(xs.shape), outs_p, outs_s, (cast[0] if next_side else None)


def kernel(x_prompt, x_sample, cache_k, cache_v, state_lru_h, state_lru_conv, state_cm_conv, state_sc_conv, page_table, norm_w, ffn1_w_gu, ffn1_w_down, ffn2_w_gu, ffn2_w_down, w_in, w_out, attn_lambda, attn_subln_w, lru_conv_w, lru_conv_b, lru_gate_w, lru_gate_b, lru_a_param, cm_conv_w, cm_conv_b, cm_ln_w, cm_ln_b, sc_conv_w):
    b, t, d = x_prompt.shape
    db, ts, _ = x_sample.shape
    depth = norm_w.shape[0]
    w = d // N_GROUPS
    n_heads = cache_k.shape[3]
    e = cache_k.shape[4]
    head_dim = e // 2
    assert n_heads * e == w and w_in.shape[2] == N_IN_SLOTS * w
    n_pool, page = cache_k.shape[1], cache_k.shape[2]
    past = page_table.shape[1] * page
    dims = (w, n_heads, head_dim)

    P = {
        'norm_w': norm_w,
        'ffn1_w_gu': ffn1_w_gu, 'ffn1_w_down': ffn1_w_down, 'ffn2_w_gu': ffn2_w_gu,
        'ffn2_w_down': ffn2_w_down, 'w_in': w_in, 'w_out': w_out,
        'attn_lambda': attn_lambda,
        'attn_subln_w': attn_subln_w.reshape(depth, 1, e),
        'lru_conv_w': lru_conv_w, 'lru_conv_b': lru_conv_b, 'lru_gate_w': lru_gate_w,
        'lru_gate_b': lru_gate_b, 'lru_a_param': lru_a_param,
        'cm_conv_w': cm_conv_w, 'cm_conv_b': cm_conv_b, 'cm_ln_w': cm_ln_w, 'cm_ln_b': cm_ln_b,
        'sc_conv_w': sc_conv_w,
    }
    cache_k4 = cache_k.reshape(depth, n_pool, page * n_heads, e)
    cache_v4 = cache_v.reshape(depth, n_pool, page * n_heads, e)
    tables_p = _rope_tables(jnp.arange(t), w, head_dim)
    tables_s = _rope_tables(past + jnp.arange(ts), w, head_dim)
    zeros_p = (jnp.zeros((b, 1, w), F32), jnp.zeros((b, lru_conv_w.shape[1] - 1, w), F32),
               jnp.zeros((b, cm_conv_w.shape[1] - 1, w), F32), jnp.zeros((b, sc_conv_w.shape[1] - 1, w), F32))

    xp, xs = x_prompt, x_sample
    outs_p, outs_s = [], []
    gu1 = ffn1_w_gu[0].astype(BF16)
    for l in range(depth):
        lam_init = 0.8 - 0.6 * math.exp(-0.3 * l)
        st_s = (state_lru_h[l].reshape(db, 1, w), state_lru_conv[l], state_cm_conv[l], state_sc_conv[l])
        xp, xs, sp, ss, gu1 = _layer(xp, xs, (tables_p, tables_s), (zeros_p, st_s), P, gu1, l, lam_init, dims,
                                     (cache_k4, cache_v4, page_table), l + 1 if l + 1 < depth else None)
        outs_p.append(sp)
        outs_s.append(ss)
    stack = lambda outs, i: jnp.stack([o[i] for o in outs])
    return (xp, xs) + tuple(stack(outs_p, i) for i in range(6)) + tuple(stack(outs_s, i) for i in range(6))
```

```python
import functools
import math

import jax
import jax.numpy as jnp
from jax import lax
from jax.experimental import pallas as pl
from jax.experimental.pallas import tpu as pltpu

F32 = jnp.float32
BF16 = jnp.bfloat16

NORM_EPS = 1e-6
ROPE_THETA = 10000.0
LRU_C = 8.0
N_GROUPS = 4
N_IN_SLOTS = 10
VMEM_LIMIT_BYTES = 56 * 1024 * 1024
LANES = 128
SUBLANES = 8
BF16_ROWS = 16
ROW_TILE = 1024
COL_TILE = 512
COL_TILE_WIDE = 1024
ROW_TILE_RESIDENT = 512
ROW_TILE_RESIDENT_LONG_K = 256
LONG_K = 4096
SUB_TILE_ROWS = 512
ATTN_TILE = 256
ROPE_ROWS = 512
MIXER_ROWS = 512
CONV_ROW_BLOCK = 64
PAGES_PER_STEP = 32


def _cparams(*sem):
    return pltpu.CompilerParams(dimension_semantics=sem, vmem_limit_bytes=VMEM_LIMIT_BYTES)


def _pick_tile(n, target, quantum):
    if n <= target:
        return n
    t = (target // quantum) * quantum
    while t >= quantum:
        if n % t == 0:
            return t
        t -= quantum
    return n


def _dot(a, b):
    return jnp.dot(a, b, preferred_element_type=F32)


def _dot_nt(a, b):
    return lax.dot_general(a, b, (((1,), (1,)), ((), ())), preferred_element_type=F32)


def _rms_rows(x, w_row):
    ms = jnp.mean(x * x, axis=-1, keepdims=True)
    return x * lax.rsqrt(ms + NORM_EPS) * w_row


def _side_cast_plan(srcs, n_steps, step_index):
    in_specs, out_specs, out_shapes = [], [], []
    for src, layer in srcs:
        _, r, c = src.shape
        nb = next(n for n in range(min(n_steps, r // BF16_ROWS), 0, -1) if r % n == 0 and (r // n) % BF16_ROWS == 0)
        rb = r // nb

        def in_map(*ids, layer=layer, nb=nb):
            return (layer, jnp.minimum(step_index(*ids), nb - 1), 0)

        def out_map(*ids, nb=nb):
            return (jnp.minimum(step_index(*ids), nb - 1), 0)

        in_specs.append(pl.BlockSpec((None, rb, c), in_map))
        out_specs.append(pl.BlockSpec((rb, c), out_map))
        out_shapes.append(jax.ShapeDtypeStruct((r, c), BF16))
    return in_specs, out_specs, out_shapes


def _run_side_casts(src_refs, dst_refs):
    for src_ref, dst_ref in zip(src_refs, dst_refs):
        dst_ref[...] = src_ref[...].astype(BF16)


def _norm_matmul_kernel(x_ref, xr_ref, nw_ref, *refs, norm_row, swiglu, n_side, tm):
    n_w = 2 if swiglu else 1
    w_refs = refs[:n_w]
    side_src = refs[n_w:n_w + n_side]
    out_ref, outr_ref = refs[n_w + n_side:n_w + n_side + 2]
    side_dst = refs[n_w + n_side + 2:n_w + 2 * n_side + 2]
    xn_ref = refs[-1]
    i = pl.program_id(0)
    j = pl.program_id(1)
    rows_all = xn_ref.shape[0]
    nw_row = nw_ref[norm_row:norm_row + 1, :]

    @pl.when(j == 0)
    def _():
        xn_ref[0:tm, :] = _rms_rows(x_ref[...], nw_row).astype(BF16)

    @pl.when((j == 0) & (i == 0))
    def _():
        xn_ref[tm:rows_all, :] = _rms_rows(xr_ref[...], nw_row).astype(BF16)

    def product(rows):
        xn = xn_ref[0:rows, :]
        if swiglu:
            g = _dot(xn, w_refs[0][...])
            u = _dot(xn, w_refs[1][...])
            return g * jax.nn.sigmoid(g) * u
        return _dot(xn, w_refs[0][...])

    @pl.when(i == 0)
    def _():
        r = product(rows_all)
        out_ref[...] = r[0:tm, :].astype(out_ref.dtype)
        outr_ref[...] = r[tm:rows_all, :].astype(outr_ref.dtype)
        _run_side_casts(side_src, side_dst)

    @pl.when(i != 0)
    def _():
        out_ref[...] = product(tm).astype(out_ref.dtype)
        _run_side_casts(side_src, side_dst)


def _norm_matmul(x, x_rider, norm_w, w, layer, norm_row, *, swiglu, out_dtype, tn_target=COL_TILE, side=()):
    m, d = x.shape
    mr = x_rider.shape[0]
    n = w.shape[1]
    n_out = n // 2 if swiglu else n
    tm = _pick_tile(m, ROW_TILE, BF16_ROWS)
    tn = _pick_tile(n_out, tn_target, LANES)
    n_blocks = n_out // tn
    grid = (m // tm, n_blocks)
    in_specs = [
        pl.BlockSpec((tm, d), lambda i, j: (i, 0)),
        pl.BlockSpec((mr, d), lambda i, j: (0, 0)),
        pl.BlockSpec((None, norm_w.shape[1], d), lambda i, j: (layer, 0, 0)),
        pl.BlockSpec((d, tn), lambda i, j: (0, j)),
    ]
    args = [x, x_rider, norm_w, w]
    if swiglu:
        in_specs.append(pl.BlockSpec((d, tn), lambda i, j: (0, j + n_blocks)))
        args.append(w)
    s_in, s_out, s_shapes = _side_cast_plan(side, grid[0] * grid[1], lambda i, j: i * n_blocks + j)
    rider_spec = pl.BlockSpec((mr, tn), lambda i, j: (0, jnp.where(i == 0, j, n_blocks - 1)))
    outs = pl.pallas_call(
        functools.partial(_norm_matmul_kernel, norm_row=norm_row, swiglu=swiglu, n_side=len(side), tm=tm),
        grid=grid,
        in_specs=in_specs + s_in,
        out_specs=[pl.BlockSpec((tm, tn), lambda i, j: (i, j)), rider_spec] + s_out,
        out_shape=[jax.ShapeDtypeStruct((m, n_out), out_dtype), jax.ShapeDtypeStruct((mr, n_out), out_dtype)]
                  + s_shapes,
        scratch_shapes=[pltpu.VMEM((tm + mr, d), BF16)],
        compiler_params=_cparams("arbitrary", "arbitrary"),
        name="norm_matmul_swiglu" if swiglu else "norm_matmul",
    )(*args, *[src for src, _ in side])
    return outs[0], outs[1], outs[2:]


def _matmul_norm_res_kernel(*refs, n_parts, part_rows, norm_row, scale, n_side, tm, n_sub):
    h_refs = refs[:n_parts]
    hr_refs = refs[n_parts:2 * n_parts]
    w_ref, x_ref, xr_ref, nw_ref = refs[2 * n_parts:2 * n_parts + 4]
    base = 2 * n_parts + 4
    side_src = refs[base:base + n_side]
    out_ref, outr_ref = refs[base + n_side:base + n_side + 2]
    side_dst = refs[base + n_side + 2:base + 2 * n_side + 2]
    lhs_refs = refs[base + 2 * n_side + 2:]
    i = pl.program_id(0)
    nw_row = nw_ref[norm_row:norm_row + 1, :]

    def product(lhs_list):
        y = None
        row = 0
        for lhs, rows in zip(lhs_list, part_rows):
            part = _dot(lhs, w_ref[row:row + rows, :])
            y = part if y is None else y + part
            row += rows
        return _rms_rows(y, nw_row)

    @pl.when(i == 0)
    def _():
        for h_ref, hr_ref, lhs_ref in zip(h_refs, hr_refs, lhs_refs):
            lhs_ref[0:tm, :] = h_ref[...]
            lhs_ref[tm:lhs_ref.shape[0], :] = hr_ref[...]
        yn = product([lhs_ref[...] for lhs_ref in lhs_refs])
        out_ref[...] = x_ref[...] + scale * yn[0:tm, :]
        outr_ref[...] = xr_ref[...] + scale * yn[tm:yn.shape[0], :]
        _run_side_casts(side_src, side_dst)

    @pl.when(i != 0)
    def _():
        sub = tm // n_sub
        for r0 in range(0, tm, sub):
            out_ref[r0:r0 + sub, :] = (x_ref[r0:r0 + sub, :]
                                       + scale * product([h_ref[r0:r0 + sub, :] for h_ref in h_refs]))
        _run_side_casts(side_src, side_dst)


def _matmul_norm_res(h_parts, h_rider_parts, w, x, x_rider, norm_w, layer, norm_row, scale, side=()):
    m, d = x.shape
    mr = x_rider.shape[0]
    k = w.shape[0]
    part_rows = tuple(h.shape[1] for h in h_parts)
    assert sum(part_rows) == k
    tm = _pick_tile(m, ROW_TILE_RESIDENT_LONG_K if k > LONG_K else ROW_TILE_RESIDENT, BF16_ROWS)
    in_specs = [pl.BlockSpec((tm, r), lambda i: (i, 0)) for r in part_rows]
    in_specs += [pl.BlockSpec((mr, r), lambda i: (0, 0)) for r in part_rows]
    in_specs += [
        pl.BlockSpec((k, d), lambda i: (0, 0), pipeline_mode=pl.Buffered(1)),
        pl.BlockSpec((tm, d), lambda i: (i, 0)),
        pl.BlockSpec((mr, d), lambda i: (0, 0)),
        pl.BlockSpec((None, norm_w.shape[1], d), lambda i: (layer, 0, 0)),
    ]
    s_in, s_out, s_shapes = _side_cast_plan(side, m // tm, lambda i: i)
    outs = pl.pallas_call(
        functools.partial(_matmul_norm_res_kernel, n_parts=len(h_parts), part_rows=part_rows,
                          norm_row=norm_row, scale=scale, n_side=len(side), tm=tm,
                          n_sub=2 if tm % SUB_TILE_ROWS == 0 else 1),
        grid=(m // tm,),
        in_specs=in_specs + s_in,
        out_specs=[pl.BlockSpec((tm, d), lambda i: (i, 0)), pl.BlockSpec((mr, d), lambda i: (0, 0))] + s_out,
        out_shape=[jax.ShapeDtypeStruct((m, d), F32), jax.ShapeDtypeStruct((mr, d), F32)] + s_shapes,
        scratch_shapes=[pltpu.VMEM((tm + mr, r), BF16) for r in part_rows],
        compiler_params=_cparams("arbitrary"),
        name="matmul_norm_res",
    )(*h_parts, *h_rider_parts, w, x, x_rider, norm_w, *[src for src, _ in side])
    return outs[0], outs[1], outs[2:]


def _swap_halves(x, half):
    n = x.shape[-1]
    lane = lax.broadcasted_iota(jnp.int32, x.shape, x.ndim - 1)
    first = (lane & half) == 0
    return jnp.where(first, pltpu.roll(x, n - half, x.ndim - 1), pltpu.roll(x, half, x.ndim - 1))


def _rope_kernel(q_ref, k_ref, v_ref, cos_ref, sin_ref, qo_ref, ko_ref, vo_ref, kb_ref, vb_ref,
                 *, half, q_scale, n_heads):
    cos = cos_ref[...]
    sin = sin_ref[...]
    q = q_ref[...]
    k = k_ref[...]
    v = v_ref[...]
    qr = q * cos + _swap_halves(q, half) * sin
    kr = k * cos + _swap_halves(k, half) * sin
    qo_ref[...] = (qr * q_scale).astype(qo_ref.dtype)
    kb_ref[...] = kr.astype(BF16)
    vb_ref[...] = v.astype(BF16)
    tt = k.shape[0]
    e = k.shape[1] // n_heads
    for h in range(n_heads):
        ko_ref[pl.ds(h, tt, stride=n_heads), :] = kr[:, h * e:(h + 1) * e]
        vo_ref[pl.ds(h, tt, stride=n_heads), :] = v[:, h * e:(h + 1) * e]


def _rope(z, cos, sin, *, w, n_heads, head_dim, q_dtype):
    b, t, _ = z.shape
    e = w // n_heads
    tt = _pick_tile(t, ROPE_ROWS, BF16_ROWS)
    zspec = lambda slot: pl.BlockSpec((None, tt, w), lambda bi, ti: (bi, ti, slot))
    tspec = pl.BlockSpec((tt, w), lambda bi, ti: (ti, 0))
    ospec = pl.BlockSpec((None, tt, w), lambda bi, ti: (bi, ti, 0))
    rspec = pl.BlockSpec((None, tt * n_heads, e), lambda bi, ti: (bi, ti, 0))
    rows_shape = jax.ShapeDtypeStruct((b, t * n_heads, e), F32)
    return pl.pallas_call(
        functools.partial(_rope_kernel, half=head_dim // 2, q_scale=head_dim ** -0.5, n_heads=n_heads),
        grid=(b, t // tt),
        in_specs=[zspec(0), zspec(1), zspec(2), tspec, tspec],
        out_specs=[ospec, rspec, rspec, ospec, ospec],
        out_shape=[jax.ShapeDtypeStruct((b, t, w), q_dtype), rows_shape, rows_shape,
                   jax.ShapeDtypeStruct((b, t, w), BF16), jax.ShapeDtypeStruct((b, t, w), BF16)],
        compiler_params=_cparams("parallel", "parallel"),
        name="rope",
    )(z, z, z, cos, sin)


def _rope_tables(pos, w, head_dim):
    half = head_dim // 2
    inv = ROPE_THETA ** (-jnp.arange(half, dtype=F32) / half)
    ang = pos.astype(F32)[:, None] * inv[None, :]
    cos = jnp.cos(ang)
    sin = jnp.sin(ang)
    cos_d = jnp.concatenate([cos, cos], axis=-1)
    sin_d = jnp.concatenate([-sin, sin], axis=-1)
    reps = w // head_dim
    return jnp.tile(cos_d, (1, reps)), jnp.tile(sin_d, (1, reps))


def _lambda_value(lp, lam_init):
    s1 = jnp.sum(lp[0:1, :] * lp[1:2, :], axis=-1, keepdims=True)
    s2 = jnp.sum(lp[2:3, :] * lp[3:4, :], axis=-1, keepdims=True)
    return jnp.exp(s1) - jnp.exp(s2) + lam_init


def _attn_prompt_kernel(q_ref, k_ref, v_ref, lp_ref, sw_ref, o_ref, *, tile, head_dim, lam_init):
    t, e = q_ref.shape
    lam = _lambda_value(lp_ref[...], lam_init)
    lane = lax.broadcasted_iota(jnp.int32, (tile, e), 1)
    row = lax.broadcasted_iota(jnp.int32, (tile, tile), 0)
    col = lax.broadcasted_iota(jnp.int32, (tile, tile), 1)
    keep = col <= row
    keep2 = jnp.concatenate([keep, keep], axis=0)
    for i in range(t // tile):
        kv = (i + 1) * tile
        q = q_ref[i * tile:kv, :]
        ks = k_ref[0:kv, :]
        vx = jnp.concatenate([v_ref[0:kv, :], jnp.ones((kv, e), BF16)], axis=1)
        qc = jnp.concatenate([jnp.where(lane < head_dim, q, jnp.zeros_like(q)),
                              jnp.where(lane >= head_dim, q, jnp.zeros_like(q))], axis=0)
        s = _dot_nt(qc, ks)
        s_diag = jnp.where(keep2, s[:, i * tile:], -jnp.inf)
        m = jnp.max(s_diag, axis=-1, keepdims=True)
        if i > 0:
            s_full = s[:, :i * tile]
            m = jnp.maximum(m, jnp.max(s_full, axis=-1, keepdims=True))
            p = jnp.concatenate([jnp.exp(s_full - m), jnp.exp(s_diag - m)], axis=1)
        else:
            p = jnp.exp(s_diag - m)
        acc = _dot(p.astype(BF16), vx)
        d = acc[:, :e] / acc[:, e:]
        normed = [d[:tile], d[tile:]]
        o = normed[0] - lam * normed[1]
        o_ref[i * tile:kv, :] = (_rms_rows(o, sw_ref[...]) * (1.0 - lam_init)).astype(o_ref.dtype)


def _attn_prompt(q, k, v, attn_lambda, subln_w, layer, lam_init, *, n_heads, head_dim):
    b, t, w = q.shape
    e = 2 * head_dim
    tile = _pick_tile(t, ATTN_TILE, LANES)
    head_spec = pl.BlockSpec((None, t, e), lambda bi, h: (bi, 0, h))
    return pl.pallas_call(
        functools.partial(_attn_prompt_kernel, tile=tile, head_dim=head_dim, lam_init=lam_init),
        grid=(b, n_heads),
        in_specs=[
            head_spec, head_spec, head_spec,
            pl.BlockSpec((None, 4, head_dim), lambda bi, h: (layer, 0, 0)),
            pl.BlockSpec((None, 1, e), lambda bi, h: (layer, 0, 0)),
        ],
        out_specs=head_spec,
        out_shape=jax.ShapeDtypeStruct((b, t, w), BF16),
        compiler_params=_cparams("parallel", "parallel"),
        name="attn_prompt",
    )(q, k, v, attn_lambda, subln_w)


def _attn_paged_kernel(pt_ref, q_ref, kn_ref, vn_ref, lp_ref, sw_ref, ck_ref, cv_ref, o_ref,
                       kbuf, vbuf, sem, qb_ref, m_ref, l_ref, acc_ref,
                       *, layer, pps, n_heads, head_dim, ts, lam_init):
    step = pl.program_id(1)
    n_steps = pl.num_programs(1)
    flat = pl.program_id(0) * n_steps + step
    total = pl.num_programs(0) * n_steps
    slot = flat % 2
    e = 2 * head_dim
    half = n_heads * ts
    rows = 2 * half
    page_rows = kbuf.shape[2]

    def page_copies(b_idx, s_idx, slot_idx):
        out = []
        for r in range(pps):
            page_id = pt_ref[b_idx, s_idx * pps + r]
            out.append(pltpu.make_async_copy(ck_ref.at[layer, page_id], kbuf.at[slot_idx, r], sem.at[slot_idx, 0]))
            out.append(pltpu.make_async_copy(cv_ref.at[layer, page_id], vbuf.at[slot_idx, r], sem.at[slot_idx, 1]))
        return out

    @pl.when(flat == 0)
    def _():
        for c in page_copies(0, 0, 0):
            c.start()

    nxt = jnp.where(flat + 1 < total, flat + 1, 0)
    for c in page_copies(nxt // n_steps, nxt % n_steps, 1 - slot):
        c.start()
    for c in page_copies(pl.program_id(0), step, slot):
        c.wait()

    row_i = lax.broadcasted_iota(jnp.int32, (rows, e), 0)
    lane_i = lax.broadcasted_iota(jnp.int32, (rows, e), 1)
    row_c = row_i // half
    row_h = (row_i % half) // ts
    row_t = row_i % ts

    @pl.when(step == 0)
    def _():
        own_comp = (lane_i // head_dim) == row_c
        qb = jnp.zeros((rows, e), F32)
        for h in range(n_heads):
            for t in range(ts):
                qb = jnp.where(own_comp & (row_h == h) & (row_t == t), q_ref[t:t + 1, h * e:(h + 1) * e], qb)
        qb_ref[...] = qb
        m_ref[...] = jnp.full(m_ref.shape, -jnp.inf, F32)
        l_ref[...] = jnp.zeros(l_ref.shape, F32)
        acc_ref[...] = jnp.zeros(acc_ref.shape, F32)

    qb = qb_ref[...]
    qb16 = qb.astype(BF16)
    col_h = lax.broadcasted_iota(jnp.int32, (rows, page_rows), 1) % n_heads
    own_cols = col_h == (lax.broadcasted_iota(jnp.int32, (rows, page_rows), 0) % half) // ts
    s = [jnp.where(own_cols, _dot_nt(qb16, kbuf[slot, r].astype(BF16)), -jnp.inf) for r in range(pps)]
    m_old = m_ref[...]
    m_new = m_old
    for r in range(pps):
        m_new = jnp.maximum(m_new, jnp.max(s[r], axis=-1, keepdims=True))
    alpha = jnp.exp(m_old - m_new)
    l_new = alpha * l_ref[...]
    acc = alpha * acc_ref[...]
    for r in range(pps):
        p = jnp.exp(s[r] - m_new)
        l_new = l_new + jnp.sum(p, axis=-1, keepdims=True)
        acc = acc + _dot(p.astype(BF16), vbuf[slot, r].astype(BF16))
    m_ref[...] = m_new
    l_ref[...] = l_new
    acc_ref[...] = acc

    @pl.when(flat == total - 1)
    def _():
        for c in page_copies(0, 0, 1 - slot):
            c.wait()

    @pl.when(step == n_steps - 1)
    def _():
        m = m_ref[...]
        l = l_ref[...]
        a = acc_ref[...]

        def own_head_rows(ref, t):
            out = jnp.zeros((rows, e), F32)
            for h in range(n_heads):
                out = jnp.where(row_h == h, ref[t:t + 1, h * e:(h + 1) * e], out)
            return out

        row_t1 = row_t[:, 0:1]
        s_new = []
        for t in range(ts):
            s_t = jnp.sum(qb * own_head_rows(kn_ref, t), axis=-1, keepdims=True)
            s_new.append(jnp.where(row_t1 >= t, s_t, -jnp.inf))
        m_fin = m
        for t in range(ts):
            m_fin = jnp.maximum(m_fin, s_new[t])
        alpha_f = jnp.exp(m - m_fin)
        l = alpha_f * l
        a = alpha_f * a
        for t in range(ts):
            p_t = jnp.exp(s_new[t] - m_fin)
            l = l + p_t
            a = a + p_t * own_head_rows(vn_ref, t)
        d = a / l
        lam = _lambda_value(lp_ref[...], lam_init)
        o = d[0:half, :] - lam * d[half:rows, :]
        on = _rms_rows(o, sw_ref[...]) * (1.0 - lam_init)
        for h in range(n_heads):
            o_ref[:, h * e:(h + 1) * e] = on[h * ts:(h + 1) * ts, :].astype(o_ref.dtype)


def _attn_paged(q, k_new, v_new, cache_k, cache_v, page_table, attn_lambda, subln_w, layer, lam_init,
                *, n_heads, head_dim):
    db, ts, w = q.shape
    e = 2 * head_dim
    page_rows = cache_k.shape[2]
    n_pages = page_table.shape[1]
    pps = _pick_tile(n_pages, PAGES_PER_STEP, 1)
    rows = 2 * n_heads * ts
    tok_spec = pl.BlockSpec((None, ts, w), lambda b, s, pt: (b, 0, 0))
    grid_spec = pltpu.PrefetchScalarGridSpec(
        num_scalar_prefetch=1,
        grid=(db, n_pages // pps),
        in_specs=[tok_spec, tok_spec, tok_spec,
                  pl.BlockSpec((None, 4, head_dim), lambda b, s, pt: (layer, 0, 0)),
                  pl.BlockSpec((None, 1, e), lambda b, s, pt: (layer, 0, 0)),
                  pl.BlockSpec(memory_space=pl.ANY), pl.BlockSpec(memory_space=pl.ANY)],
        out_specs=pl.BlockSpec((None, ts, w), lambda b, s, pt: (b, 0, 0)),
        scratch_shapes=[pltpu.VMEM((2, pps, page_rows, e), F32), pltpu.VMEM((2, pps, page_rows, e), F32),
                        pltpu.SemaphoreType.DMA((2, 2)),
                        pltpu.VMEM((rows, e), F32), pltpu.VMEM((rows, 1), F32),
                        pltpu.VMEM((rows, 1), F32), pltpu.VMEM((rows, e), F32)],
    )
    return pl.pallas_call(
        functools.partial(_attn_paged_kernel, layer=layer, pps=pps, n_heads=n_heads, head_dim=head_dim, ts=ts,
                          lam_init=lam_init),
        grid_spec=grid_spec,
        out_shape=jax.ShapeDtypeStruct((db, ts, w), BF16),
        compiler_params=_cparams("arbitrary", "arbitrary"),
        name="attn_paged",
    )(page_table, q, k_new, v_new, attn_lambda, subln_w, cache_k, cache_v)


def _gelu_tanh(x):
    return 0.5 * x * (1.0 + jnp.tanh(math.sqrt(2.0 / math.pi) * (x + 0.044715 * (x * x * x))))


def _conv_from_buffer(buf_ref, w_ref, *, halo, width, rows, row_block):
    first = halo - (width - 1)
    outs = []
    for r0 in range(0, rows, row_block):
        rb = min(row_block, rows - r0)
        acc = None
        for phase in range(min(SUBLANES, width)):
            taps = list(range(phase, width, SUBLANES))
            span = rb + (taps[-1] - phase)
            zs = buf_ref[pl.ds(first + r0 + phase, span), :]
            part = None
            for j in taps:
                term = w_ref[j:j + 1, :] * zs[j - phase:j - phase + rb, :]
                part = term if part is None else part + term
            acc = part if acc is None else acc + part
        outs.append(acc)
    return outs[0] if len(outs) == 1 else jnp.concatenate(outs, axis=0)


def _mixers_kernel(xb_ref, gb_ref, ca_ref, cb_ref, sb_ref, sc_ref, sx_ref,
                   h0_ref, lru_st_ref, cm_st_ref, sc_st_ref,
                   lcw_ref, lcb_ref, gw_ref, gbias_ref, ap_ref,
                   ccw_ref, ccb_ref, lnw_ref, lnb_ref, scw_ref,
                   y_ref, h_out_ref, lru_out_ref, cm_out_ref, sc_out_ref,
                   lru_buf, cm_buf, sc_buf, h_ref, hs_buf,
                   *, tt, n_t, halos, widths, blk):
    ti = pl.program_id(1)
    w = xb_ref.shape[-1]
    (halo_l, halo_c, halo_s), (wid_l, wid_c, wid_s) = halos, widths

    @pl.when(ti == 0)
    def _():
        lru_buf[0:halo_l, :] = lru_st_ref[...]
        cm_buf[0:halo_c, :] = cm_st_ref[...]
        sc_buf[0:halo_s, :] = sc_st_ref[...]
        h_ref[...] = h0_ref[...]

    lru_buf[halo_l:halo_l + tt, :] = xb_ref[...]
    xc = _conv_from_buffer(lru_buf, lcw_ref, halo=halo_l, width=wid_l, rows=tt, row_block=tt) + lcb_ref[...]
    xc16 = xc.astype(BF16)
    gates = []
    for g in range(2):
        parts = [_dot(xc16[:, n * blk:(n + 1) * blk], gw_ref[g, n].astype(BF16)) for n in range(w // blk)]
        gates.append(jnp.concatenate(parts, axis=1) + gbias_ref[g:g + 1, :])
    r_gate = jax.nn.sigmoid(gates[0])
    i_gate = jax.nn.sigmoid(gates[1])
    log_a = -LRU_C * r_gate * jax.nn.softplus(-ap_ref[...])
    a = jnp.exp(log_a)
    u = jnp.sqrt(1.0 - jnp.exp(2.0 * log_a)) * (i_gate * xc)
    if tt % SUBLANES == 0:
        groups = tt // SUBLANES
        a3 = a.reshape(groups, SUBLANES, w)
        u3 = u.reshape(groups, SUBLANES, w)
        sub = lax.broadcasted_iota(jnp.int32, (groups, SUBLANES, w), 1)
        dist = 1
        while dist < SUBLANES:
            live = sub >= dist
            a_prev = jnp.where(live, pltpu.roll(a3, dist, 1), 1.0)
            u_prev = jnp.where(live, pltpu.roll(u3, dist, 1), 0.0)
            u3 = a3 * u_prev + u3
            a3 = a3 * a_prev
            dist *= 2
        h_last = h_ref[...]
        blocks = []
        for g in range(groups):
            blk_h = u3[g] + a3[g] * h_last
            blocks.append(blk_h)
            h_last = blk_h[SUBLANES - 1:SUBLANES, :]
        hs = jnp.concatenate(blocks, axis=0)
    else:
        h_last = h_ref[...]
        for r in range(tt):
            h_last = a[r:r + 1, :] * h_last + u[r:r + 1, :]
            hs_buf[r:r + 1, :] = h_last
        hs = hs_buf[0:tt, :]
    h_ref[...] = h_last
    y_ref[:, 0:w] = (_gelu_tanh(gb_ref[...]) * hs).astype(y_ref.dtype)

    cm_buf[halo_c:halo_c + tt, :] = ca_ref[...] * jax.nn.sigmoid(cb_ref[...])
    uc = _conv_from_buffer(cm_buf, ccw_ref, halo=halo_c, width=wid_c, rows=tt, row_block=CONV_ROW_BLOCK) + ccb_ref[...]
    mu = jnp.mean(uc, axis=-1, keepdims=True)
    cen = uc - mu
    var = jnp.mean(cen * cen, axis=-1, keepdims=True)
    ln = cen * lax.rsqrt(var + NORM_EPS) * lnw_ref[...] + lnb_ref[...]
    y_ref[:, w:2 * w] = (ln * jax.nn.sigmoid(ln)).astype(y_ref.dtype)

    sc_buf[halo_s:halo_s + tt, :] = sc_ref[...] * sx_ref[...]
    sconv = _conv_from_buffer(sc_buf, scw_ref, halo=halo_s, width=wid_s, rows=tt, row_block=tt)
    y_ref[:, 2 * w:3 * w] = (sb_ref[...] * sconv).astype(y_ref.dtype)

    @pl.when(ti == n_t - 1)
    def _():
        h_out_ref[...] = h_last
        lru_out_ref[...] = lru_buf[pl.ds(tt + halo_l - (wid_l - 1), wid_l - 1), :]
        cm_out_ref[...] = cm_buf[pl.ds(tt + halo_c - (wid_c - 1), wid_c - 1), :]
        sc_out_ref[...] = sc_buf[pl.ds(tt + halo_s - (wid_s - 1), wid_s - 1), :]

    if n_t > 1:
        @pl.when(ti < n_t - 1)
        def _():
            lru_buf[0:halo_l, :] = lru_buf[tt:tt + halo_l, :]
            cm_buf[0:halo_c, :] = cm_buf[tt:tt + halo_c, :]
            sc_buf[0:halo_s, :] = sc_buf[tt:tt + halo_s, :]


def _pad_state(st, halo):
    return jnp.pad(st, ((0, 0), (halo - st.shape[1], 0), (0, 0)))


def _mixers(z, states, P, layer, *, w):
    b, t, _ = z.shape
    h0, lru_st, cm_st, sc_st = states
    widths = (P['lru_conv_w'].shape[1], P['cm_conv_w'].shape[1], P['sc_conv_w'].shape[1])
    halos = tuple(-(-(wd - 1) // SUBLANES) * SUBLANES for wd in widths)
    tt = _pick_tile(t, MIXER_ROWS, SUBLANES)
    n_t = t // tt
    nb = P['lru_gate_w'].shape[2]
    blk = w // nb
    zspec = lambda slot: pl.BlockSpec((None, tt, w), lambda bi, ti: (bi, ti, slot))
    bspec = lambda rows: pl.BlockSpec((None, rows, w), lambda bi, ti: (bi, 0, 0))
    lspec = lambda rows: pl.BlockSpec((None, rows, w), lambda bi, ti: (layer, 0, 0))
    in_specs = [zspec(s) for s in range(3, 10)]
    in_specs += [bspec(1), bspec(halos[0]), bspec(halos[1]), bspec(halos[2])]
    in_specs += [lspec(widths[0]), lspec(1),
                 pl.BlockSpec((None, 2, nb, blk, blk), lambda bi, ti: (layer, 0, 0, 0, 0)),
                 lspec(2), lspec(1), lspec(widths[1]), lspec(1), lspec(1), lspec(1), lspec(widths[2])]
    out_specs = [pl.BlockSpec((None, tt, 3 * w), lambda bi, ti: (bi, ti, 0)),
                 bspec(1), bspec(widths[0] - 1), bspec(widths[1] - 1), bspec(widths[2] - 1)]
    out_shape = [jax.ShapeDtypeStruct((b, t, 3 * w), BF16),
                 jax.ShapeDtypeStruct((b, 1, w), F32),
                 jax.ShapeDtypeStruct((b, widths[0] - 1, w), F32),
                 jax.ShapeDtypeStruct((b, widths[1] - 1, w), F32),
                 jax.ShapeDtypeStruct((b, widths[2] - 1, w), F32)]
    row3 = lambda a: a.reshape(a.shape[0], 1, a.shape[1])
    return pl.pallas_call(
        functools.partial(_mixers_kernel, tt=tt, n_t=n_t, halos=halos, widths=widths, blk=blk),
        grid=(b, n_t),
        in_specs=in_specs,
        out_specs=out_specs,
        out_shape=out_shape,
        scratch_shapes=[pltpu.VMEM((halos[0] + tt, w), F32), pltpu.VMEM((halos[1] + tt, w), F32),
                        pltpu.VMEM((halos[2] + tt, w), F32), pltpu.VMEM((1, w), F32),
                        pltpu.VMEM((SUBLANES, w), F32)],
        compiler_params=_cparams("parallel", "arbitrary"),
        name="mixers",
    )(*([z] * 7), h0, _pad_state(lru_st, halos[0]), _pad_state(cm_st, halos[1]), _pad_state(sc_st, halos[2]),
      P['lru_conv_w'], row3(P['lru_conv_b']), P['lru_gate_w'], P['lru_gate_b'], row3(P['lru_a_param']),
      P['cm_conv_w'], row3(P['cm_conv_b']), row3(P['cm_ln_w']), row3(P['cm_ln_b']), P['sc_conv_w'])


def _group_mixing(z, x_shape, pos_tables, states, P, layer, lam_init, dims, paged=None):
    b, t, _ = x_shape
    w, n_heads, head_dim = dims
    e = 2 * head_dim
    z = z.reshape(b, t, N_IN_SLOTS * w)
    cos, sin = pos_tables
    q, k_rows, v_rows, k16, v16 = _rope(z, cos, sin, w=w, n_heads=n_heads, head_dim=head_dim,
                                        q_dtype=BF16 if paged is None else F32)
    if paged is None:
        ya = _attn_prompt(q, k16, v16, P['attn_lambda'], P['attn_subln_w'], layer, lam_init,
                          n_heads=n_heads, head_dim=head_dim)
    else:
        cache_k, cache_v, page_table = paged
        ya = _attn_paged(q, k_rows.reshape(b, t, w), v_rows.reshape(b, t, w), cache_k, cache_v, page_table,
                         P['attn_lambda'], P['attn_subln_w'], layer, lam_init, n_heads=n_heads, head_dim=head_dim)
    ybcd, h_t, lru_buf, cm_buf, sc_buf = _mixers(z, states, P, layer, w=w)
    outs = (k_rows.reshape(b, t, n_heads, e), v_rows.reshape(b, t, n_heads, e), h_t.reshape(b, w),
            lru_buf, cm_buf, sc_buf)
    return ya.reshape(b * t, w), ybcd.reshape(b * t, 3 * w), outs


def _layer(xp, xs, tables, states, P, gu1, layer, lam_init, dims, paged, cast_next):
    nw = P['norm_w']
    d = xp.shape[-1]
    p2 = xp.reshape(-1, d)
    s2 = xs.reshape(-1, d)
    side = lambda *names: [(P[n], layer) for n in names]
    hp, hs, (wd1, w_in) = _norm_matmul(p2, s2, nw, gu1, layer, 0, swiglu=True, out_dtype=BF16,
                                       side=side('ffn1_w_down', 'w_in'))
    p2, s2, (gu2,) = _matmul_norm_res([hp], [hs], wd1, p2, s2, nw, layer, 1, 0.5, side=side('ffn2_w_gu'))
    zp, zs, (w_out,) = _norm_matmul(p2, s2, nw, w_in, layer, 2, swiglu=False, out_dtype=F32, tn_target=COL_TILE_WIDE,
                                    side=side('w_out'))
    ya_p, ybcd_p, outs_p = _group_mixing(zp, xp.shape, tables[0], states[0], P, layer, lam_init, dims)
    ya_s, ybcd_s, outs_s = _group_mixing(zs, xs.shape, tables[1], states[1], P, layer, lam_init, dims, paged=paged)
    p2, s2, _ = _matmul_norm_res([ya_p, ybcd_p], [ya_s, ybcd_s], w_out, p2, s2, nw, layer, 3, 1.0)
    hp, hs, (wd2,) = _norm_matmul(p2, s2, nw, gu2, layer, 4, swiglu=True, out_dtype=BF16, side=side('ffn2_w_down'))
    next_side = [(P['ffn1_w_gu'], cast_next)] if cast_next is not None else []
    p2, s2, cast = _matmul_norm_res([hp], [hs], wd2, p2, s2, nw, layer, 5, 0.5, side=next_side)
    return p2.reshape(xp.shape), s2.reshape(xs.shape), outs_p, outs_s, (cast[0] if next_side else None)


def kernel(x_prompt, x_sample, cache_k, cache_v, state_lru_h, state_lru_conv, state_cm_conv, state_sc_conv, page_table, norm_w, ffn1_w_gu, ffn1_w_down, ffn2_w_gu, ffn2_w_down, w_in, w_out, attn_lambda, attn_subln_w, lru_conv_w, lru_conv_b, lru_gate_w, lru_gate_b, lru_a_param, cm_conv_w, cm_conv_b, cm_ln_w, cm_ln_b, sc_conv_w):
    b, t, d = x_prompt.shape
    db, ts, _ = x_sample.shape
    depth = norm_w.shape[0]
    w = d // N_GROUPS
    n_heads = cache_k.shape[3]
    e = cache_k.shape[4]
    head_dim = e // 2
    assert n_heads * e == w and w_in.shape[2] == N_IN_SLOTS * w
    n_pool, page = cache_k.shape[1], cache_k.shape[2]
    past = page_table.shape[1] * page
    dims = (w, n_heads, head_dim)

    P = {
        'norm_w': norm_w,
        'ffn1_w_gu': ffn1_w_gu, 'ffn1_w_down': ffn1_w_down, 'ffn2_w_gu': ffn2_w_gu,
        'ffn2_w_down': ffn2_w_down, 'w_in': w_in, 'w_out': w_out,
        'attn_lambda': attn_lambda,
        'attn_subln_w': attn_subln_w.reshape(depth, 1, e),
        'lru_conv_w': lru_conv_w, 'lru_conv_b': lru_conv_b, 'lru_gate_w': lru_gate_w,
        'lru_gate_b': lru_gate_b, 'lru_a_param': lru_a_param,
        'cm_conv_w': cm_conv_w, 'cm_conv_b': cm_conv_b, 'cm_ln_w': cm_ln_w, 'cm_ln_b': cm_ln_b,
        'sc_conv_w': sc_conv_w,
    }
    cache_k4 = cache_k.reshape(depth, n_pool, page * n_heads, e)
    cache_v4 = cache_v.reshape(depth, n_pool, page * n_heads, e)
    tables_p = _rope_tables(jnp.arange(t), w, head_dim)
    tables_s = _rope_tables(past + jnp.arange(ts), w, head_dim)
    zeros_p = (jnp.zeros((b, 1, w), F32), jnp.zeros((b, lru_conv_w.shape[1] - 1, w), F32),
               jnp.zeros((b, cm_conv_w.shape[1] - 1, w), F32), jnp.zeros((b, sc_conv_w.shape[1] - 1, w), F32))

    xp, xs = x_prompt, x_sample
    outs_p, outs_s = [], []
    gu1 = ffn1_w_gu[0].astype(BF16)
    for l in range(depth):
        lam_init = 0.8 - 0.6 * math.exp(-0.3 * l)
        st_s = (state_lru_h[l].reshape(db, 1, w), state_lru_conv[l], state_cm_conv[l], state_sc_conv[l])
        xp, xs, sp, ss, gu1 = _layer(xp, xs, (tables_p, tables_s), (zeros_p, st_s), P, gu1, l, lam_init, dims,
                                     (cache_k4, cache_v4, page_table), l + 1 if l + 1 < depth else None)
        outs_p.append(sp)
        outs_s.append(ss)
    stack = lambda outs, i: jnp.stack([o[i] for o in outs])
    return (xp, xs) + tuple(stack(outs_p, i) for i in range(6)) + tuple(stack(outs_s, i) for i in range(6))
```
